```python
import math
import jax, jax.numpy as jnp
from jax import lax
import numpy as np

D_MODEL = 2048
BATCH = 1
SEQ = 16384
DEPTH = 4

HEAD_DIM = 128
N_MIX_HEADS = 8
GDN_QK_HEADS = 4
MIX_WIDTH = N_MIX_HEADS * HEAD_DIM
GDN_QK_WIDTH = GDN_QK_HEADS * HEAD_DIM
N_MEM = 256
MEM_HEADS = 4
MEM_WIDTH = MEM_HEADS * HEAD_DIM
CONV_WIDTH = 4
GDN_CHUNK = 64
SB_BLOCK = 128
SB_SUB = 32
PEER_HEADS = 4
PEER_NKEYS = 128
PEER_EXPERTS = PEER_NKEYS * PEER_NKEYS
PEER_QDIM = 256
PEER_HALF = PEER_QDIM // 2
PEER_HALF_TOPK = 16
PEER_TOPK = 8
PEER_BLOCK = 128
N_MIXERS = 2
N_GDN_LAYERS = (DEPTH + 1) // 2
N_SB_LAYERS = DEPTH // 2
DN_ALPHA = (2.0 * DEPTH) ** 0.25
DN_BETA = (8.0 * DEPTH) ** -0.25
LN_EPS = 1e-5
RMS_EPS = 1e-6
GDN_CONV_CH = 2 * GDN_QK_WIDTH + MIX_WIDTH
GDN_IN = 2 * GDN_QK_WIDTH + 2 * MIX_WIDTH + 2 * N_MIX_HEADS + MEM_WIDTH
SB_IN = 3 * MIX_WIDTH + MEM_WIDTH

kernel_name = 'hybrid_gdn_stickbreak_peer_deepnorm'


def layer_norm(x, g, b):
    xf = x.astype(jnp.float32)
    mu = jnp.mean(xf, -1, keepdims=True)
    var = jnp.mean(jnp.square(xf - mu), -1, keepdims=True)
    y = (xf - mu) * lax.rsqrt(var + LN_EPS) * g.astype(jnp.float32) + b.astype(jnp.float32)
    return y.astype(x.dtype)


def rms_norm(x, g):
    xf = x.astype(jnp.float32)
    return xf * lax.rsqrt(jnp.mean(xf * xf, -1, keepdims=True) + RMS_EPS) * g.astype(jnp.float32)


def l2norm(x):
    xf = x.astype(jnp.float32)
    return xf * lax.rsqrt(jnp.sum(xf * xf, -1, keepdims=True) + RMS_EPS)


def causal_conv(x, w):
    return lax.conv_general_dilated(x, w[:, None, :], window_strides=(1,), padding=[(CONV_WIDTH - 1, 0)],
                                    dimension_numbers=('NWC', 'WIO', 'NWC'), feature_group_count=x.shape[-1])


def gated_deltanet(q, k, v, z, b, a, conv_w, a_log, dt_bias, onorm_g):
    B_, S_, _ = q.shape
    f32 = jnp.float32
    C = GDN_CHUNK
    qkv = jax.nn.silu(causal_conv(jnp.concatenate([q, k, v], -1), conv_w))
    q, k, v = jnp.split(qkv, [GDN_QK_WIDTH, 2 * GDN_QK_WIDTH], axis=-1)
    rep = N_MIX_HEADS // GDN_QK_HEADS
    q = jnp.repeat(l2norm(q.reshape(B_, S_, GDN_QK_HEADS, HEAD_DIM)), rep, axis=2) * (HEAD_DIM ** -0.5)
    k = jnp.repeat(l2norm(k.reshape(B_, S_, GDN_QK_HEADS, HEAD_DIM)), rep, axis=2)
    v = v.reshape(B_, S_, N_MIX_HEADS, HEAD_DIM).astype(f32)
    beta = jax.nn.sigmoid(b.astype(f32))
    g = -jnp.exp(a_log.astype(f32)) * jax.nn.softplus(a.astype(f32) + dt_bias.astype(f32))
    n = S_ // C

    def chunks(t):
        return jnp.moveaxis(t.reshape(B_, n, C, *t.shape[2:]), 3, 1)

    qc, kc, vc, bc, gc = chunks(q), chunks(k), chunks(v), chunks(beta), chunks(g)
    gcum = jnp.cumsum(gc, axis=-1)
    glast = gcum[..., -1]
    causal = jnp.tril(jnp.ones((C, C), bool))
    strict = jnp.tril(jnp.ones((C, C), bool), -1)
    decay = jnp.exp(jnp.where(causal, gcum[..., :, None] - gcum[..., None, :], -jnp.inf))
    kb = kc * bc[..., None]
    m_low = jnp.where(strict, jnp.einsum('bhncd,bhnsd->bhncs', kb, kc) * decay, 0.0)
    eye = jnp.eye(C, dtype=f32)
    rhs = jnp.concatenate([vc * bc[..., None], kb * jnp.exp(gcum)[..., None]], -1)
    sol = lax.linalg.triangular_solve(eye + m_low, rhs, left_side=True, lower=True, unit_diagonal=True)
    u, w = sol[..., :HEAD_DIM], sol[..., HEAD_DIM:]
    aqk = jnp.einsum('bhncd,bhnsd->bhncs', qc, kc) * decay
    qg = qc * jnp.exp(gcum)[..., None]
    kd = kc * jnp.exp(glast[..., None] - gcum)[..., None]

    def step(state, inp):
        qg_n, kd_n, u_n, w_n, aqk_n, gl_n = inp
        v_new = u_n - jnp.einsum('bhck,bhkv->bhcv', w_n, state)
        o = jnp.einsum('bhck,bhkv->bhcv', qg_n, state) + jnp.einsum('bhcs,bhsv->bhcv', aqk_n, v_new)
        state = state * jnp.exp(gl_n)[..., None, None] + jnp.einsum('bhck,bhcv->bhkv', kd_n, v_new)
        return state, o

    xs = tuple(jnp.moveaxis(t, 2, 0) for t in (qg, kd, u, w, aqk, glast))
    s0 = jnp.zeros((B_, N_MIX_HEADS, HEAD_DIM, HEAD_DIM), f32)
    _, o = lax.scan(step, s0, xs)
    o = jnp.moveaxis(o, 0, 2).transpose(0, 2, 3, 1, 4).reshape(B_, S_, N_MIX_HEADS, HEAD_DIM)
    o = rms_norm(o, onorm_g) * jax.nn.silu(z.reshape(B_, S_, N_MIX_HEADS, HEAD_DIM).astype(f32))
    return o.reshape(B_, S_, MIX_WIDTH).astype(z.dtype)


def stick_breaking(q, k, v):
    B_, S_, _ = q.shape
    f32 = jnp.float32
    hi = lax.Precision.HIGHEST
    q = q.reshape(B_, S_, N_MIX_HEADS, HEAD_DIM).astype(f32) * (HEAD_DIM ** -0.5)
    k = k.reshape(B_, S_, N_MIX_HEADS, HEAD_DIM).astype(f32)
    vf = v.reshape(B_, S_, N_MIX_HEADS, HEAD_DIM).astype(f32)
    nb = S_ // SB_BLOCK
    tri_sub = (jnp.arange(SB_SUB)[:, None] > jnp.arange(SB_SUB)[None, :]).astype(f32)
    outs = []
    for blk in range(nb):
        L = (blk + 1) * SB_BLOCK
        n_sub = L // SB_SUB
        q_blk = q[:, blk * SB_BLOCK:(blk + 1) * SB_BLOCK]
        zs = jnp.einsum('bqhd,bshd->bhqs', q_blk, k[:, :L])
        q_pos = blk * SB_BLOCK + jnp.arange(SB_BLOCK)
        mask = jnp.arange(L)[None, :] < q_pos[:, None]
        zs = jnp.where(mask, zs, -jnp.inf)
        log_not = jax.nn.log_sigmoid(-zs)
        ln_sub = log_not.reshape(B_, N_MIX_HEADS, SB_BLOCK, n_sub, SB_SUB)
        within = jnp.einsum('bhqnj,js->bhqns', ln_sub, tri_sub, precision=hi)
        tri_blk = (jnp.arange(n_sub)[:, None] > jnp.arange(n_sub)[None, :]).astype(f32)
        offs = jnp.einsum('bhqm,mn->bhqn', jnp.sum(ln_sub, -1), tri_blk, precision=hi)
        later = (within + offs[..., None]).reshape(B_, N_MIX_HEADS, SB_BLOCK, L)
        att = jnp.exp(log_not + zs + later)
        outs.append(jnp.einsum('bhqs,bshd->bqhd', att, vf[:, :L]))
    o = jnp.concatenate(outs, axis=1)
    return o.reshape(B_, S_, MIX_WIDTH).astype(v.dtype)


def memory_attention(qm, mem, w_kv):
    B_, S_, _ = qm.shape
    km, vm = jnp.split(mem @ w_kv, 2, axis=-1)
    q = qm.reshape(B_, S_, MEM_HEADS, HEAD_DIM)
    km = km.reshape(B_, N_MEM, MEM_HEADS, HEAD_DIM)
    vm = vm.reshape(B_, N_MEM, MEM_HEADS, HEAD_DIM)
    s = jnp.einsum('bqhd,bmhd->bhqm', q, km).astype(jnp.float32) * (HEAD_DIM ** -0.5)
    p = jax.nn.softmax(s, axis=-1)
    o = jnp.einsum('bhqm,bmhd->bqhd', p.astype(vm.dtype), vm)
    return o.reshape(B_, S_, MEM_WIDTH)


def peer(x, w_q, sub_keys, u_tab, v_tab):
    B_, S_, D = x.shape
    f32 = jnp.float32
    T = B_ * S_
    xt = x.reshape(T, D)
    qry = (xt @ w_q).reshape(T, PEER_HEADS, 2, PEER_HALF).astype(f32)
    s = jnp.einsum('thpc,pnc->thpn', qry, sub_keys.astype(f32))
    sv, si = lax.top_k(s, PEER_HALF_TOPK)
    cand = (sv[:, :, 0, :, None] + sv[:, :, 1, None, :]).reshape(T, PEER_HEADS, PEER_HALF_TOPK * PEER_HALF_TOPK)
    cv, ci = lax.top_k(cand, PEER_TOPK)
    i1 = jnp.take_along_axis(si[:, :, 0], ci // PEER_HALF_TOPK, axis=-1)
    i2 = jnp.take_along_axis(si[:, :, 1], ci % PEER_HALF_TOPK, axis=-1)
    expert = (i1 * PEER_NKEYS + i2).reshape(T, PEER_HEADS * PEER_TOPK)
    gate = jax.nn.softmax(cv, axis=-1).reshape(T, PEER_HEADS * PEER_TOPK)
    nb = T // PEER_BLOCK

    def block(args):
        xb, eb, gb = args
        h = jnp.einsum('td,ted->te', xb, u_tab[eb])
        act = (jax.nn.gelu(h.astype(f32), approximate=False) * gb).astype(x.dtype)
        return jnp.einsum('te,ted->td', act, v_tab[eb])

    out = lax.map(block, (xt.reshape(nb, PEER_BLOCK, D), expert.reshape(nb, PEER_BLOCK, -1),
                          gate.reshape(nb, PEER_BLOCK, -1)))
    return out.reshape(B_, S_, D)


def setup_inputs(seed: int = 0) -> dict:
    key = jax.random.key(seed)
    ks = jax.random.split(key, 20)
    f32 = jnp.float32
    D = D_MODEL

    def nrm(k, shape, scale):
        return jax.random.normal(k, shape, f32) * scale

    x = nrm(ks[0], (BATCH, SEQ, D), 1.0)
    mem = nrm(ks[1], (BATCH, N_MEM, D), 1.0)
    v0 = 2 * GDN_QK_WIDTH
    gdn_scale = jnp.ones((GDN_IN,), f32).at[v0:v0 + MIX_WIDTH].set(DN_BETA)
    gdn_w_in = nrm(ks[2], (N_GDN_LAYERS, D, GDN_IN), D ** -0.5) * gdn_scale
    gdn_conv = nrm(ks[3], (N_GDN_LAYERS, CONV_WIDTH, GDN_CONV_CH), CONV_WIDTH ** -0.5)
    gdn_a_log = jnp.log(jax.random.uniform(ks[4], (N_GDN_LAYERS, N_MIX_HEADS), f32, 1.0, 16.0))
    dt = jnp.exp(jax.random.uniform(ks[5], (N_GDN_LAYERS, N_MIX_HEADS), f32, math.log(1e-3), math.log(1e-1)))
    gdn_dt_bias = dt + jnp.log(-jnp.expm1(-dt))
    gdn_onorm = 1.0 + nrm(ks[6], (N_GDN_LAYERS, HEAD_DIM), 0.02)
    sb_scale = jnp.ones((SB_IN,), f32).at[2 * MIX_WIDTH:3 * MIX_WIDTH].set(DN_BETA)
    sb_w_in = nrm(ks[7], (N_SB_LAYERS, D, SB_IN), D ** -0.5) * sb_scale
    mem_scale = jnp.ones((2 * MEM_WIDTH,), f32).at[MEM_WIDTH:].set(DN_BETA)
    mem_w_kv = nrm(ks[8], (DEPTH, D, 2 * MEM_WIDTH), D ** -0.5) * mem_scale
    w_out = nrm(ks[9], (DEPTH, MIX_WIDTH + MEM_WIDTH, D), (MIX_WIDTH + MEM_WIDTH) ** -0.5 * DN_BETA)
    ln_mix_g = 1.0 + nrm(ks[10], (DEPTH, D), 0.02)
    ln_mix_b = nrm(ks[11], (DEPTH, D), 0.02)
    peer_w_q = nrm(ks[12], (DEPTH, D, PEER_HEADS * PEER_QDIM), D ** -0.5)
    peer_keys = nrm(ks[13], (DEPTH, 2, PEER_NKEYS, PEER_HALF), PEER_HALF ** -0.5)
    peer_u = nrm(ks[14], (DEPTH, PEER_EXPERTS, D), D ** -0.5)
    peer_v = nrm(ks[15], (DEPTH, PEER_EXPERTS, D), DN_BETA * PEER_HEADS ** -0.5)
    ln_ffn_g = 1.0 + nrm(ks[16], (DEPTH, D), 0.02)
    ln_ffn_b = nrm(ks[17], (DEPTH, D), 0.02)
    return {'x': x, 'mem': mem, 'gdn_w_in': gdn_w_in, 'gdn_conv': gdn_conv, 'gdn_a_log': gdn_a_log,
            'gdn_dt_bias': gdn_dt_bias, 'gdn_onorm': gdn_onorm, 'sb_w_in': sb_w_in, 'mem_w_kv': mem_w_kv,
            'w_out': w_out, 'ln_mix_g': ln_mix_g, 'ln_mix_b': ln_mix_b, 'peer_w_q': peer_w_q,
            'peer_keys': peer_keys, 'peer_u': peer_u, 'peer_v': peer_v, 'ln_ffn_g': ln_ffn_g, 'ln_ffn_b': ln_ffn_b}


def reference(x, mem, gdn_w_in, gdn_conv, gdn_a_log, gdn_dt_bias, gdn_onorm, sb_w_in, mem_w_kv, w_out,
              ln_mix_g, ln_mix_b, peer_w_q, peer_keys, peer_u, peer_v, ln_ffn_g, ln_ffn_b):
    q_w, v_w = GDN_QK_WIDTH, MIX_WIDTH
    gdn_splits = [q_w, 2 * q_w, 2 * q_w + v_w, 2 * q_w + 2 * v_w, 2 * q_w + 2 * v_w + N_MIX_HEADS,
                  2 * q_w + 2 * v_w + 2 * N_MIX_HEADS]
    sb_splits = [MIX_WIDTH, 2 * MIX_WIDTH, 3 * MIX_WIDTH]
    for i in range(DEPTH):
        li = i // N_MIXERS
        if i % N_MIXERS == 0:
            p = x @ gdn_w_in[li]
            q, k, v, z, b, a, qm = jnp.split(p, gdn_splits, axis=-1)
            mix = gated_deltanet(q, k, v, z, b, a, gdn_conv[li], gdn_a_log[li], gdn_dt_bias[li], gdn_onorm[li])
        else:
            p = x @ sb_w_in[li]
            q, k, v, qm = jnp.split(p, sb_splits, axis=-1)
            mix = stick_breaking(q, k, v)
        mem_o = memory_attention(qm, mem, mem_w_kv[i])
        y = jnp.concatenate([mix, mem_o], axis=-1) @ w_out[i]
        x = layer_norm(DN_ALPHA * x + y, ln_mix_g[i], ln_mix_b[i])
        f = peer(x, peer_w_q[i], peer_keys[i], peer_u[i], peer_v[i])
        x = layer_norm(DN_ALPHA * x + f, ln_ffn_g[i], ln_ffn_b[i])
    return x
```

```python
import functools
import math

import jax
import jax.numpy as jnp
from jax import lax
from jax.experimental import pallas as pl
from jax.experimental.pallas import tpu as pltpu

F32 = jnp.float32
BF16 = jnp.bfloat16
HI = lax.Precision.HIGHEST

D_MODEL = 2048
DEPTH = 4
HEAD_DIM = 128
N_MIX_HEADS = 8
GDN_QK_HEADS = 4
MIX_WIDTH = N_MIX_HEADS * HEAD_DIM
GDN_QK_WIDTH = GDN_QK_HEADS * HEAD_DIM
N_MEM = 256
MEM_HEADS = 4
MEM_WIDTH = MEM_HEADS * HEAD_DIM
CONV_WIDTH = 4
GDN_CHUNK = 64
SB_BLOCK = 128
PEER_HEADS = 4
PEER_NKEYS = 128
PEER_EXPERTS = PEER_NKEYS * PEER_NKEYS
PEER_HALF = 128
PEER_HALF_TOPK = 16
PEER_TOPK = 8
PEER_SLOTS = PEER_HEADS * PEER_TOPK
DN_ALPHA = (2.0 * DEPTH) ** 0.25
LN_EPS = 1e-5
RMS_EPS = 1e-6
LANES = 128
SUBLANES = 8
EXP_ZERO_BELOW = -104.0

GDN_P_WIDTH = 3840
GDN_COL_Z = 2048
GDN_COL_QM = 3072
GDN_COL_B = 3584
GDN_COL_A = 3712
SB_P_WIDTH = 3 * MIX_WIDTH + MEM_WIDTH
SB_COL_QM = 3 * MIX_WIDTH

VMEM_LIMIT = 56 * 1024 * 1024


def _cparams(*sem):
    return pltpu.CompilerParams(dimension_semantics=sem, vmem_limit_bytes=VMEM_LIMIT)


def _dot(a, b):
    return jnp.dot(a, b, preferred_element_type=F32)


def _dot_nt(a, b):
    return lax.dot_general(a, b, (((1,), (1,)), ((), ())), preferred_element_type=F32)


def _dot_tn(a, b):
    return lax.dot_general(a, b, (((0,), (0,)), ((), ())), preferred_element_type=F32)


def _dot_hi(a, b):
    return jnp.dot(a, b, precision=HI, preferred_element_type=F32)


def _softplus(x):
    return jnp.maximum(x, 0.0) + jnp.log1p(jnp.exp(-jnp.abs(x)))


def _silu(x):
    return x * jax.nn.sigmoid(x)


def _layer_norm(r, g, b):
    mu = jnp.mean(r, axis=-1, keepdims=True)
    d = r - mu
    var = jnp.mean(d * d, axis=-1, keepdims=True)
    return d * lax.rsqrt(var + LN_EPS) * g + b


def _mm_kernel(x_ref, w_ref, o_ref):
    o_ref[...] = _dot(x_ref[...].astype(BF16), w_ref[...])


def _matmul(x, w, tm, tn):
    m, k = x.shape
    n = w.shape[1]
    return pl.pallas_call(
        _mm_kernel,
        grid=(m // tm, n // tn),
        in_specs=[pl.BlockSpec((tm, k), lambda i, j: (i, 0)),
                  pl.BlockSpec((k, tn), lambda i, j: (0, j))],
        out_specs=pl.BlockSpec((tm, tn), lambda i, j: (i, j)),
        out_shape=jax.ShapeDtypeStruct((m, n), F32),
        compiler_params=_cparams("parallel", "parallel"),
        name="proj_matmul",
    )(x, w)


GDN_PREP_ROWS = 256


def _gdn_prep_kernel(pc_ref, ph_ref, pb_ref, pa_ref, cw_ref, alog_ref, dtb_ref,
                     q_ref, k_ref, v_ref, beta_ref, gcum_ref, xbuf):
    tb = GDN_PREP_ROWS
    first = pl.program_id(0) == 0
    halo = ph_ref[...]
    xbuf[0:SUBLANES, :] = jnp.where(first, 0.0, halo)
    xbuf[SUBLANES:, :] = pc_ref[...]
    n_groups = (2 * GDN_QK_WIDTH + MIX_WIDTH) // LANES
    for gi in range(n_groups):
        cols = slice(gi * LANES, (gi + 1) * LANES)
        y = jnp.zeros((tb, LANES), F32)
        for j in range(CONV_WIDTH):
            r0 = SUBLANES - (CONV_WIDTH - 1) + j
            tap = xbuf[r0:r0 + tb, cols]
            y = y + tap * cw_ref[j:j + 1, cols]
        y = _silu(y)
        if gi < 2 * GDN_QK_HEADS:
            y = y * lax.rsqrt(jnp.sum(y * y, axis=-1, keepdims=True) + RMS_EPS)
        if gi < GDN_QK_HEADS:
            q_ref[:, cols] = y * (HEAD_DIM ** -0.5)
        elif gi < 2 * GDN_QK_HEADS:
            k_ref[:, (gi - GDN_QK_HEADS) * LANES:(gi - GDN_QK_HEADS + 1) * LANES] = y
        else:
            v_ref[:, (gi - 2 * GDN_QK_HEADS) * LANES:(gi - 2 * GDN_QK_HEADS + 1) * LANES] = y
    beta_ref[...] = jax.nn.sigmoid(pb_ref[...])
    g = -jnp.exp(alog_ref[...]) * _softplus(pa_ref[...] + dtb_ref[...])
    r = lax.broadcasted_iota(jnp.int32, (tb, tb), 0)
    c = lax.broadcasted_iota(jnp.int32, (tb, tb), 1)
    shift = GDN_CHUNK.bit_length() - 1
    same = lax.shift_right_logical(r, shift) == lax.shift_right_logical(c, shift)
    tri = jnp.where(same & (c <= r), 1.0, 0.0).astype(F32)
    gcum_ref[...] = _dot_hi(tri, g)


def _gdn_prep(p, conv_w, alog_pad, dtb_pad):
    t = p.shape[0]
    tb = GDN_PREP_ROWS
    cw = 2 * GDN_QK_WIDTH + MIX_WIDTH
    per8 = tb // SUBLANES
    return pl.pallas_call(
        _gdn_prep_kernel,
        grid=(t // tb,),
        in_specs=[
            pl.BlockSpec((tb, cw), lambda i: (i, 0)),
            pl.BlockSpec((SUBLANES, cw), lambda i: (jnp.maximum(i * per8 - 1, 0), 0)),
            pl.BlockSpec((tb, LANES), lambda i: (i, GDN_COL_B // LANES)),
            pl.BlockSpec((tb, LANES), lambda i: (i, GDN_COL_A // LANES)),
            pl.BlockSpec((CONV_WIDTH, cw), lambda i: (0, 0)),
            pl.BlockSpec((1, LANES), lambda i: (0, 0)),
            pl.BlockSpec((1, LANES), lambda i: (0, 0)),
        ],
        out_specs=[
            pl.BlockSpec((tb, GDN_QK_WIDTH), lambda i: (i, 0)),
            pl.BlockSpec((tb, GDN_QK_WIDTH), lambda i: (i, 0)),
            pl.BlockSpec((tb, MIX_WIDTH), lambda i: (i, 0)),
            pl.BlockSpec((tb, LANES), lambda i: (i, 0)),
            pl.BlockSpec((tb, LANES), lambda i: (i, 0)),
        ],
        out_shape=[
            jax.ShapeDtypeStruct((t, GDN_QK_WIDTH), F32),
            jax.ShapeDtypeStruct((t, GDN_QK_WIDTH), F32),
            jax.ShapeDtypeStruct((t, MIX_WIDTH), F32),
            jax.ShapeDtypeStruct((t, LANES), F32),
            jax.ShapeDtypeStruct((t, LANES), F32),
        ],
        scratch_shapes=[pltpu.VMEM((tb + SUBLANES, cw), F32)],
        compiler_params=_cparams("parallel"),
        name="gdn_prep",
    )(p, p, p, p, conv_w, alog_pad, dtb_pad)


def _gdn_chunk_kernel(q_ref, k_ref, v_ref, beta_ref, gcum_ref,
                      u_ref, w_ref, qg_ref, kd_ref, aqk_ref, egl_ref):
    c = GDN_CHUNK
    beta = beta_ref[...]
    gcum = gcum_ref[...]
    gcum_t = gcum.T
    glast_col = gcum_t[:, c - 1:c]
    egl_ref[...] = jnp.exp(jnp.broadcast_to(glast_col, (LANES, LANES))[0:N_MIX_HEADS, :])
    ri = lax.broadcasted_iota(jnp.int32, (c, c), 0)
    ci = lax.broadcasted_iota(jnp.int32, (c, c), 1)
    causal = ci <= ri
    strict = ci < ri
    rep = N_MIX_HEADS // GDN_QK_HEADS
    for hq in range(GDN_QK_HEADS):
        qh = q_ref[:, hq * LANES:(hq + 1) * LANES]
        kh = k_ref[:, hq * LANES:(hq + 1) * LANES]
        kh_b = kh.astype(BF16)
        qk = _dot_nt(qh.astype(BF16), kh_b)
        for j in range(rep):
            h = hq * rep + j
            hd = slice(h * LANES, (h + 1) * LANES)
            g_col = gcum[:, h:h + 1]
            g_row = gcum_t[h:h + 1, :]
            b_col = beta[:, h:h + 1]
            g_last = gcum[c - 1:c, h:h + 1]
            decay = jnp.exp(jnp.where(causal, g_col - g_row, -jnp.inf))
            eg = jnp.exp(g_col)
            kb = kh * b_col
            kk = _dot_nt(kb.astype(BF16), kh_b)
            nm = jnp.where(strict, -(kk * decay), 0.0)
            sol = jnp.concatenate([v_ref[:, hd] * b_col, kb * eg], axis=1)
            pw = nm
            for step in range(6):
                sol = sol + _dot_hi(pw, sol)
                if step < 5:
                    pw = _dot_hi(pw, pw)
            u_ref[:, hd] = sol[:, :LANES]
            w_ref[:, hd] = sol[:, LANES:].astype(BF16)
            qg_ref[:, hd] = (qh * eg).astype(BF16)
            kd_ref[:, hd] = (kh * jnp.exp(g_last - g_col)).astype(BF16)
            aqk_ref[0, h] = (qk * decay).astype(BF16)


def _gdn_chunk(qn, kn, vc, beta, gcum):
    t = qn.shape[0]
    c = GDN_CHUNK
    n = t // c
    row = lambda w: pl.BlockSpec((c, w), lambda i: (i, 0))
    return pl.pallas_call(
        _gdn_chunk_kernel,
        grid=(n,),
        in_specs=[row(GDN_QK_WIDTH), row(GDN_QK_WIDTH), row(MIX_WIDTH), row(LANES), row(LANES)],
        out_specs=[row(MIX_WIDTH), row(MIX_WIDTH), row(MIX_WIDTH), row(MIX_WIDTH),
                   pl.BlockSpec((1, N_MIX_HEADS, c, c), lambda i: (i, 0, 0, 0)),
                   pl.BlockSpec((N_MIX_HEADS, LANES), lambda i: (i, 0))],
        out_shape=[
            jax.ShapeDtypeStruct((t, MIX_WIDTH), F32),
            jax.ShapeDtypeStruct((t, MIX_WIDTH), BF16),
            jax.ShapeDtypeStruct((t, MIX_WIDTH), BF16),
            jax.ShapeDtypeStruct((t, MIX_WIDTH), BF16),
            jax.ShapeDtypeStruct((n, N_MIX_HEADS, c, c), BF16),
            jax.ShapeDtypeStruct((n * N_MIX_HEADS, LANES), F32),
        ],
        compiler_params=_cparams("parallel"),
        name="gdn_chunk",
    )(qn, kn, vc, beta, gcum)


def _gdn_scan_kernel(u_ref, w_ref, qg_ref, kd_ref, aqk_ref, egl_ref, z_ref, on_ref,
                     o_ref, s_ref):
    @pl.when(pl.program_id(0) == 0)
    def _():
        s_ref[...] = jnp.zeros_like(s_ref)

    on = on_ref[...]
    for h in range(N_MIX_HEADS):
        hd = slice(h * LANES, (h + 1) * LANES)
        s = s_ref[h]
        s_b = s.astype(BF16)
        v_new = u_ref[:, hd] - _dot(w_ref[:, hd], s_b)
        v_b = v_new.astype(BF16)
        o = _dot(qg_ref[:, hd], s_b) + _dot(aqk_ref[0, h], v_b)
        s_ref[h] = s * egl_ref[h:h + 1, :] + _dot_tn(kd_ref[:, hd], v_b)
        o = o * lax.rsqrt(jnp.mean(o * o, axis=-1, keepdims=True) + RMS_EPS) * on
        o_ref[:, hd] = o * _silu(z_ref[:, hd])


def _gdn_scan(u, w, qg, kd, aqk, egl, p, onorm):
    t = u.shape[0]
    c = GDN_CHUNK
    n = t // c
    row = lambda wd: pl.BlockSpec((c, wd), lambda i: (i, 0))
    return pl.pallas_call(
        _gdn_scan_kernel,
        grid=(n,),
        in_specs=[row(MIX_WIDTH), row(MIX_WIDTH), row(MIX_WIDTH), row(MIX_WIDTH),
                  pl.BlockSpec((1, N_MIX_HEADS, c, c), lambda i: (i, 0, 0, 0)),
                  pl.BlockSpec((N_MIX_HEADS, LANES), lambda i: (i, 0)),
                  pl.BlockSpec((c, MIX_WIDTH), lambda i: (i, GDN_COL_Z // MIX_WIDTH)),
                  pl.BlockSpec((1, LANES), lambda i: (0, 0))],
        out_specs=row(MIX_WIDTH),
        out_shape=jax.ShapeDtypeStruct((t, MIX_WIDTH), F32),
        scratch_shapes=[pltpu.VMEM((N_MIX_HEADS, HEAD_DIM, HEAD_DIM), F32)],
        compiler_params=_cparams("arbitrary"),
        name="gdn_scan",
    )(u, w, qg, kd, aqk, egl, p, onorm)


def _sb_block(q_b, k_blk, v_blk, c_run, acc, tri, mask):
    z = _dot_nt(q_b, k_blk.astype(BF16))
    t = jnp.log1p(jnp.exp(-jnp.abs(z)))
    log_not = -jnp.maximum(z, 0.0) - t
    log_beta = jnp.minimum(z, 0.0) - t
    if mask is not None:
        log_not = jnp.where(mask, log_not, 0.0)
    suffix = _dot_hi(log_not, tri)
    att = jnp.exp(log_beta + suffix + c_run)
    if mask is not None:
        att = jnp.where(mask, att, 0.0)
    acc = acc + _dot(att.astype(BF16), v_blk.astype(BF16))
    c_run = c_run + jnp.sum(log_not, axis=-1, keepdims=True)
    return c_run, acc


def _sb_kernel(q_ref, k_ref, v_ref, o_ref):
    blk = SB_BLOCK
    qb = pl.program_id(1)
    q_b = (q_ref[...] * (HEAD_DIM ** -0.5)).astype(BF16)
    ri = lax.broadcasted_iota(jnp.int32, (blk, blk), 0)
    ci = lax.broadcasted_iota(jnp.int32, (blk, blk), 1)
    tri = jnp.where(ri > ci, 1.0, 0.0).astype(F32)
    start = pl.multiple_of(qb * blk, blk)
    c0 = jnp.zeros((blk, 1), F32)
    acc0 = jnp.zeros((blk, HEAD_DIM), F32)
    c_run, acc = _sb_block(q_b, k_ref[pl.ds(start, blk), :], v_ref[pl.ds(start, blk), :],
                           c0, acc0, tri, ci < ri)

    def cond(carry):
        kb, go, _, _ = carry
        return jnp.logical_and(kb >= 0, go)

    def body(carry):
        kb, _, c_run, acc = carry
        s0 = pl.multiple_of(kb * blk, blk)
        c_run, acc = _sb_block(q_b, k_ref[pl.ds(s0, blk), :], v_ref[pl.ds(s0, blk), :],
                               c_run, acc, tri, None)
        return kb - 1, jnp.max(c_run) > EXP_ZERO_BELOW, c_run, acc

    _, _, _, acc = lax.while_loop(cond, body, (qb - 1, jnp.max(c_run) > EXP_ZERO_BELOW, c_run, acc))
    o_ref[...] = acc


def _stick_breaking(p):
    t = p.shape[0]
    blk = SB_BLOCK
    return pl.pallas_call(
        _sb_kernel,
        grid=(N_MIX_HEADS, t // blk),
        in_specs=[pl.BlockSpec((blk, HEAD_DIM), lambda h, i: (i, h)),
                  pl.BlockSpec((t, HEAD_DIM), lambda h, i: (0, N_MIX_HEADS + h)),
                  pl.BlockSpec((t, HEAD_DIM), lambda h, i: (0, 2 * N_MIX_HEADS + h))],
        out_specs=pl.BlockSpec((blk, HEAD_DIM), lambda h, i: (i, h)),
        out_shape=jax.ShapeDtypeStruct((t, MIX_WIDTH), F32),
        compiler_params=_cparams("parallel", "parallel"),
        name="stick_breaking",
    )(p, p, p)


MEM_ROWS = 512


def _mem_attn_kernel(q_ref, kv_ref, o_ref):
    for h in range(MEM_HEADS):
        hd = slice(h * LANES, (h + 1) * LANES)
        km = kv_ref[:, hd].astype(BF16)
        vm = kv_ref[:, MEM_WIDTH + h * LANES:MEM_WIDTH + (h + 1) * LANES].astype(BF16)
        s = _dot_nt(q_ref[:, hd].astype(BF16), km) * (HEAD_DIM ** -0.5)
        s = s - jnp.max(s, axis=-1, keepdims=True)
        e = jnp.exp(s)
        prob = e / jnp.sum(e, axis=-1, keepdims=True)
        o_ref[:, hd] = _dot(prob.astype(BF16), vm)


def _mem_attn(p, kv, qm_col):
    t = p.shape[0]
    tb = MEM_ROWS
    return pl.pallas_call(
        _mem_attn_kernel,
        grid=(t // tb,),
        in_specs=[pl.BlockSpec((tb, MEM_WIDTH), lambda i: (i, qm_col // MEM_WIDTH)),
                  pl.BlockSpec((N_MEM, 2 * MEM_WIDTH), lambda i: (0, 0))],
        out_specs=pl.BlockSpec((tb, MEM_WIDTH), lambda i: (i, 0)),
        out_shape=jax.ShapeDtypeStruct((t, MEM_WIDTH), F32),
        compiler_params=_cparams("parallel"),
        name="mem_attn",
    )(p, kv)


OUT_ROWS = 256


def _out_ln_kernel(mix_ref, mem_ref, x_ref, w_ref, g_ref, b_ref, o_ref):
    y = _dot(mix_ref[...].astype(BF16), w_ref[0:MIX_WIDTH, :])
    y = y + _dot(mem_ref[...].astype(BF16), w_ref[MIX_WIDTH:, :])
    o_ref[...] = _layer_norm(DN_ALPHA * x_ref[...] + y, g_ref[...], b_ref[...])


def _out_ln(mix, mem_o, x, w_out, g, b):
    t, d = x.shape
    tb = OUT_ROWS
    return pl.pallas_call(
        _out_ln_kernel,
        grid=(t // tb,),
        in_specs=[pl.BlockSpec((tb, MIX_WIDTH), lambda i: (i, 0)),
                  pl.BlockSpec((tb, MEM_WIDTH), lambda i: (i, 0)),
                  pl.BlockSpec((tb, d), lambda i: (i, 0)),
                  pl.BlockSpec((MIX_WIDTH + MEM_WIDTH, d), lambda i: (0, 0)),
                  pl.BlockSpec((1, d), lambda i: (0, 0)),
                  pl.BlockSpec((1, d), lambda i: (0, 0))],
        out_specs=pl.BlockSpec((tb, d), lambda i: (i, 0)),
        out_shape=jax.ShapeDtypeStruct((t, d), F32),
        compiler_params=_cparams("parallel"),
        name="out_proj_ln",
    )(mix, mem_o, x, w_out, g, b)


ROUTE_ROWS = 256


def _route_kernel(x_ref, wq_ref, keys_ref, ids_ref, gates_ref):
    tb = ROUTE_ROWS
    ncand = PEER_HALF_TOPK * PEER_HALF_TOPK
    qry = _dot(x_ref[...].astype(BF16), wq_ref[...])
    lane_k = lax.broadcasted_iota(jnp.int32, (tb, PEER_NKEYS), 1).astype(F32)
    lane_c = lax.broadcasted_iota(jnp.int32, (tb, ncand), 1)
    shift = PEER_HALF_TOPK.bit_length() - 1
    rank_a = lax.shift_right_logical(lane_c, shift).astype(F32)
    rank_b = (lane_c & (PEER_HALF_TOPK - 1)).astype(F32)
    lane_cf = lane_c.astype(F32)
    lane_o = lax.broadcasted_iota(jnp.int32, (tb, LANES), 1)
    ids_out = jnp.zeros((tb, LANES), F32)
    gates_out = jnp.zeros((tb, LANES), F32)
    neg = -jnp.inf
    for h in range(PEER_HEADS):
        cand = jnp.zeros((tb, ncand), F32)
        expert = jnp.zeros((tb, ncand), F32)
        for half in range(2):
            col = (h * 2 + half) * PEER_HALF
            s = _dot_nt(qry[:, col:col + PEER_HALF].astype(BF16), keys_ref[half].astype(BF16))
            rank_lane = rank_a if half == 0 else rank_b
            val = jnp.zeros((tb, ncand), F32)
            idx = jnp.zeros((tb, ncand), F32)
            for r in range(PEER_HALF_TOPK):
                m = jnp.max(s, axis=-1, keepdims=True)
                am = jnp.min(jnp.where(s == m, lane_k, float(PEER_NKEYS)), axis=-1, keepdims=True)
                hit = rank_lane == float(r)
                val = jnp.where(hit, m, val)
                idx = jnp.where(hit, am, idx)
                s = jnp.where(lane_k == am, neg, s)
            cand = cand + val
            expert = expert + idx * (float(PEER_NKEYS) if half == 0 else 1.0)
        cvs, eids = [], []
        for r in range(PEER_TOPK):
            m = jnp.max(cand, axis=-1, keepdims=True)
            am = jnp.min(jnp.where(cand == m, lane_cf, float(ncand)), axis=-1, keepdims=True)
            sel = lane_cf == am
            eids.append(jnp.max(jnp.where(sel, expert, -1.0), axis=-1, keepdims=True))
            cvs.append(m)
            cand = jnp.where(sel, neg, cand)
        exps = [jnp.exp(cv - cvs[0]) for cv in cvs]
        denom = exps[0]
        for e in exps[1:]:
            denom = denom + e
        for r in range(PEER_TOPK):
            slot = lane_o == (h * PEER_TOPK + r)
            ids_out = jnp.where(slot, eids[r], ids_out)
            gates_out = jnp.where(slot, exps[r] / denom, gates_out)
    ids_ref[...] = ids_out.astype(jnp.int32)
    gates_ref[...] = gates_out


def _route(x, wq, keys):
    t, d = x.shape
    tb = ROUTE_ROWS
    nq = wq.shape[1]
    return pl.pallas_call(
        _route_kernel,
        grid=(t // tb,),
        in_specs=[pl.BlockSpec((tb, d), lambda i: (i, 0)),
                  pl.BlockSpec((d, nq), lambda i: (0, 0)),
                  pl.BlockSpec((2, PEER_NKEYS, PEER_HALF), lambda i: (0, 0, 0))],
        out_specs=[pl.BlockSpec((tb, LANES), lambda i: (i, 0)),
                   pl.BlockSpec((tb, LANES), lambda i: (i, 0))],
        out_shape=[jax.ShapeDtypeStruct((t, LANES), jnp.int32),
                   jax.ShapeDtypeStruct((t, LANES), F32)],
        compiler_params=_cparams("parallel"),
        name="peer_route",
    )(x, wq, keys)


PEER_ROWS = 512
PEER_ECHUNK = 512


def _peer_kernel(x_ref, ids_ref, gates_ref, u_ref, v_ref, g_ref, b_ref, o_ref, acc_ref, xb_ref):
    j = pl.program_id(1)
    tb, ec = PEER_ROWS, PEER_ECHUNK

    @pl.when(j == 0)
    def _():
        xb_ref[...] = x_ref[...].astype(BF16)
        acc_ref[...] = jnp.zeros_like(acc_ref)

    hid = _dot_nt(xb_ref[...], u_ref[...])
    lane_e = lax.broadcasted_iota(jnp.int32, (tb, ec), 1) + j * ec
    ids = ids_ref[...]
    gates = gates_ref[...]
    wgt = jnp.zeros((tb, ec), F32)
    for s in range(PEER_SLOTS):
        wgt = wgt + jnp.where(ids[:, s:s + 1] == lane_e, gates[:, s:s + 1], 0.0)
    act = 0.5 * hid * (1.0 + lax.erf(hid * (2.0 ** -0.5))) * wgt
    acc_ref[...] += _dot(act.astype(BF16), v_ref[...])

    @pl.when(j == pl.num_programs(1) - 1)
    def _():
        o_ref[...] = _layer_norm(DN_ALPHA * x_ref[...] + acc_ref[...], g_ref[...], b_ref[...])


def _peer(x, ids, gates, u, v, g, b):
    t, d = x.shape
    tb, ec = PEER_ROWS, PEER_ECHUNK
    ne = u.shape[0]
    return pl.pallas_call(
        _peer_kernel,
        grid=(t // tb, ne // ec),
        in_specs=[pl.BlockSpec((tb, d), lambda i, j: (i, 0)),
                  pl.BlockSpec((tb, LANES), lambda i, j: (i, 0)),
                  pl.BlockSpec((tb, LANES), lambda i, j: (i, 0)),
                  pl.BlockSpec((ec, d), lambda i, j: (j, 0)),
                  pl.BlockSpec((ec, d), lambda i, j: (j, 0)),
                  pl.BlockSpec((1, d), lambda i, j: (0, 0)),
                  pl.BlockSpec((1, d), lambda i, j: (0, 0))],
        out_specs=pl.BlockSpec((tb, d), lambda i, j: (i, 0)),
        out_shape=jax.ShapeDtypeStruct((t, d), F32),
        scratch_shapes=[pltpu.VMEM((tb, d), F32), pltpu.VMEM((tb, d), BF16)],
        compiler_params=_cparams("parallel", "arbitrary"),
        name="peer_dense",
    )(x, ids, gates, u, v, g, b)


def _pad_lanes(vec):
    return jnp.zeros((1, LANES), F32).at[0, :vec.shape[0]].set(vec.astype(F32))


def _gdn_weight(w):
    qw, vw, nh = GDN_QK_WIDTH, MIX_WIDTH, N_MIX_HEADS
    d = w.shape[0]
    o_b = 2 * qw + 2 * vw
    main = w[:, :o_b]
    b_cols = w[:, o_b:o_b + nh]
    a_cols = w[:, o_b + nh:o_b + 2 * nh]
    qm = w[:, o_b + 2 * nh:]
    zpad = jnp.zeros((d, LANES - nh), w.dtype)
    return jnp.concatenate([main, qm, b_cols, zpad, a_cols, zpad], axis=1).astype(BF16)


def _forward(x, mem, gdn_w_in, gdn_conv, gdn_a_log, gdn_dt_bias, gdn_onorm, sb_w_in, mem_w_kv, w_out,
             ln_mix_g, ln_mix_b, peer_w_q, peer_keys, peer_u, peer_v, ln_ffn_g, ln_ffn_b):
    t = x.shape[1]
    d = x.shape[2]
    xt = x.reshape(t, d)
    memt = mem.reshape(N_MEM, d)
    row = lambda a: a.reshape(1, -1).astype(F32)
    for i in range(DEPTH):
        li = i // 2
        if i % 2 == 0:
            p = _matmul(xt, _gdn_weight(gdn_w_in[li]), 512, 768)
            qn, kn, vc, beta, gcum = _gdn_prep(p, gdn_conv[li].astype(F32),
                                               _pad_lanes(gdn_a_log[li]), _pad_lanes(gdn_dt_bias[li]))
            u, w, qg, kd, aqk, egl = _gdn_chunk(qn, kn, vc, beta, gcum)
            mix = _gdn_scan(u, w, qg, kd, aqk, egl, p, row(gdn_onorm[li]))
            qm_col = GDN_COL_QM
        else:
            p = _matmul(xt, sb_w_in[li].astype(BF16), 512, 512)
            mix = _stick_breaking(p)
            qm_col = SB_COL_QM
        kv = _matmul(memt, mem_w_kv[i].astype(BF16), N_MEM, 512)
        mem_o = _mem_attn(p, kv, qm_col)
        x1 = _out_ln(mix, mem_o, xt, w_out[i].astype(BF16), row(ln_mix_g[i]), row(ln_mix_b[i]))
        ids, gates = _route(x1, peer_w_q[i].astype(BF16), peer_keys[i].astype(F32))
        xt = _peer(x1, ids, gates, peer_u[i].astype(BF16), peer_v[i].astype(BF16),
                   row(ln_ffn_g[i]), row(ln_ffn_b[i]))
    return xt.reshape(x.shape)


def kernel(x, mem, gdn_w_in, gdn_conv, gdn_a_log, gdn_dt_bias, gdn_onorm, sb_w_in, mem_w_kv, w_out,
           ln_mix_g, ln_mix_b, peer_w_q, peer_keys, peer_u, peer_v, ln_ffn_g, ln_ffn_b):
    return _forward(x, mem, gdn_w_in, gdn_conv, gdn_a_log, gdn_dt_bias, gdn_onorm, sb_w_in, mem_w_kv,
                    w_out, ln_mix_g, ln_mix_b, peer_w_q, peer_keys, peer_u, peer_v, ln_ffn_g, ln_ffn_b)
```

```python
import functools
import math

import jax
import jax.numpy as jnp
from jax import lax
from jax.experimental import pallas as pl
from jax.experimental.pallas import tpu as pltpu

F32 = jnp.float32
BF16 = jnp.bfloat16
HI = lax.Precision.HIGHEST

D_MODEL = 2048
DEPTH = 4
HEAD_DIM = 128
N_MIX_HEADS = 8
GDN_QK_HEADS = 4
MIX_WIDTH = N_MIX_HEADS * HEAD_DIM
GDN_QK_WIDTH = GDN_QK_HEADS * HEAD_DIM
N_MEM = 256
MEM_HEADS = 4
MEM_WIDTH = MEM_HEADS * HEAD_DIM
CONV_WIDTH = 4
GDN_CHUNK = 64
SB_BLOCK = 128
PEER_HEADS = 4
PEER_NKEYS = 128
PEER_EXPERTS = PEER_NKEYS * PEER_NKEYS
PEER_HALF = 128
PEER_HALF_TOPK = 16
PEER_TOPK = 8
PEER_SLOTS = PEER_HEADS * PEER_TOPK
DN_ALPHA = (2.0 * DEPTH) ** 0.25
LN_EPS = 1e-5
RMS_EPS = 1e-6
LANES = 128
SUBLANES = 8
EXP_ZERO_BELOW = -104.0

GDN_P_WIDTH = 3840
GDN_COL_Z = 2048
GDN_COL_QM = 3072
GDN_COL_B = 3584
GDN_COL_A = 3712
SB_P_WIDTH = 3 * MIX_WIDTH + MEM_WIDTH
SB_COL_QM = 3 * MIX_WIDTH

VMEM_LIMIT = 56 * 1024 * 1024


def _cparams(*sem):
    return pltpu.CompilerParams(dimension_semantics=sem, vmem_limit_bytes=VMEM_LIMIT)


def _dot(a, b):
    return jnp.dot(a, b, preferred_element_type=F32)


def _dot_nt(a, b):
    return lax.dot_general(a, b, (((1,), (1,)), ((), ())), preferred_element_type=F32)


def _dot_tn(a, b):
    return lax.dot_general(a, b, (((0,), (0,)), ((), ())), preferred_element_type=F32)


def _dot_hi(a, b):
    return jnp.dot(a, b, precision=HI, preferred_element_type=F32)


def _split2(a):
    hi = a.astype(BF16)
    lo = (a - hi.astype(F32)).astype(BF16)
    return hi, lo


def _dot3(a2, b2):
    return _dot(a2[0], b2[0]) + (_dot(a2[0], b2[1]) + _dot(a2[1], b2[0]))


def _softplus(x):
    return jnp.maximum(x, 0.0) + jnp.log1p(jnp.exp(-jnp.abs(x)))


def _silu(x):
    return x * jax.nn.sigmoid(x)


def _layer_norm(r, g, b):
    mu = jnp.mean(r, axis=-1, keepdims=True)
    d = r - mu
    var = jnp.mean(d * d, axis=-1, keepdims=True)
    return d * lax.rsqrt(var + LN_EPS) * g + b


def _mm_kernel(x_ref, w_ref, o_ref):
    o_ref[...] = _dot(x_ref[...].astype(BF16), w_ref[...])


def _matmul(x, w, tm, tn):
    m, k = x.shape
    n = w.shape[1]
    return pl.pallas_call(
        _mm_kernel,
        grid=(m // tm, n // tn),
        in_specs=[pl.BlockSpec((tm, k), lambda i, j: (i, 0)),
                  pl.BlockSpec((k, tn), lambda i, j: (0, j))],
        out_specs=pl.BlockSpec((tm, tn), lambda i, j: (i, j)),
        out_shape=jax.ShapeDtypeStruct((m, n), F32),
        compiler_params=_cparams("parallel", "parallel"),
        name="proj_matmul",
    )(x, w)


GDN_PREP_ROWS = 256


def _gdn_prep_kernel(pc_ref, ph_ref, pb_ref, pa_ref, cw_ref, alog_ref, dtb_ref,
                     q_ref, k_ref, v_ref, beta_ref, gcum_ref, xbuf):
    tb = GDN_PREP_ROWS
    first = pl.program_id(0) == 0
    halo = ph_ref[...]
    xbuf[0:SUBLANES, :] = jnp.where(first, 0.0, halo)
    xbuf[SUBLANES:, :] = pc_ref[...]
    n_groups = (2 * GDN_QK_WIDTH + MIX_WIDTH) // LANES
    for gi in range(n_groups):
        cols = slice(gi * LANES, (gi + 1) * LANES)
        y = jnp.zeros((tb, LANES), F32)
        for j in range(CONV_WIDTH):
            r0 = SUBLANES - (CONV_WIDTH - 1) + j
            tap = xbuf[r0:r0 + tb, cols]
            y = y + tap * cw_ref[j:j + 1, cols]
        y = _silu(y)
        if gi < 2 * GDN_QK_HEADS:
            y = y * lax.rsqrt(jnp.sum(y * y, axis=-1, keepdims=True) + RMS_EPS)
        if gi < GDN_QK_HEADS:
            q_ref[:, cols] = y * (HEAD_DIM ** -0.5)
        elif gi < 2 * GDN_QK_HEADS:
            k_ref[:, (gi - GDN_QK_HEADS) * LANES:(gi - GDN_QK_HEADS + 1) * LANES] = y
        else:
            v_ref[:, (gi - 2 * GDN_QK_HEADS) * LANES:(gi - 2 * GDN_QK_HEADS + 1) * LANES] = y
    beta_ref[...] = jax.nn.sigmoid(pb_ref[...])
    g = -jnp.exp(alog_ref[...]) * _softplus(pa_ref[...] + dtb_ref[...])
    r = lax.broadcasted_iota(jnp.int32, (tb, tb), 0)
    c = lax.broadcasted_iota(jnp.int32, (tb, tb), 1)
    shift = GDN_CHUNK.bit_length() - 1
    same = lax.shift_right_logical(r, shift) == lax.shift_right_logical(c, shift)
    tri = jnp.where(same & (c <= r), 1.0, 0.0).astype(F32)
    gcum_ref[...] = _dot_hi(tri, g)


def _gdn_prep(p, conv_w, alog_pad, dtb_pad):
    t = p.shape[0]
    tb = GDN_PREP_ROWS
    cw = 2 * GDN_QK_WIDTH + MIX_WIDTH
    per8 = tb // SUBLANES
    return pl.pallas_call(
        _gdn_prep_kernel,
        grid=(t // tb,),
        in_specs=[
            pl.BlockSpec((tb, cw), lambda i: (i, 0)),
            pl.BlockSpec((SUBLANES, cw), lambda i: (jnp.maximum(i * per8 - 1, 0), 0)),
            pl.BlockSpec((tb, LANES), lambda i: (i, GDN_COL_B // LANES)),
            pl.BlockSpec((tb, LANES), lambda i: (i, GDN_COL_A // LANES)),
            pl.BlockSpec((CONV_WIDTH, cw), lambda i: (0, 0)),
            pl.BlockSpec((1, LANES), lambda i: (0, 0)),
            pl.BlockSpec((1, LANES), lambda i: (0, 0)),
        ],
        out_specs=[
            pl.BlockSpec((tb, GDN_QK_WIDTH), lambda i: (i, 0)),
            pl.BlockSpec((tb, GDN_QK_WIDTH), lambda i: (i, 0)),
            pl.BlockSpec((tb, MIX_WIDTH), lambda i: (i, 0)),
            pl.BlockSpec((tb, LANES), lambda i: (i, 0)),
            pl.BlockSpec((tb, LANES), lambda i: (i, 0)),
        ],
        out_shape=[
            jax.ShapeDtypeStruct((t, GDN_QK_WIDTH), F32),
            jax.ShapeDtypeStruct((t, GDN_QK_WIDTH), F32),
            jax.ShapeDtypeStruct((t, MIX_WIDTH), F32),
            jax.ShapeDtypeStruct((t, LANES), F32),
            jax.ShapeDtypeStruct((t, LANES), F32),
        ],
        scratch_shapes=[pltpu.VMEM((tb + SUBLANES, cw), F32)],
        compiler_params=_cparams("parallel"),
        name="gdn_prep",
    )(p, p, p, p, conv_w, alog_pad, dtb_pad)


def _gdn_chunk_kernel(q_ref, k_ref, v_ref, beta_ref, gcum_ref,
                      u_ref, w_ref, qg_ref, kd_ref, aqk_ref, egl_ref):
    c = GDN_CHUNK
    beta = beta_ref[...]
    gcum = gcum_ref[...]
    gcum_t = gcum.T
    glast_col = gcum_t[:, c - 1:c]
    egl_ref[...] = jnp.exp(jnp.broadcast_to(glast_col, (LANES, LANES))[0:N_MIX_HEADS, :])
    ri = lax.broadcasted_iota(jnp.int32, (c, c), 0)
    ci = lax.broadcasted_iota(jnp.int32, (c, c), 1)
    causal = ci <= ri
    strict = ci < ri
    eye = jnp.where(ci == ri, 1.0, 0.0).astype(F32)
    rep = N_MIX_HEADS // GDN_QK_HEADS
    heads = range(N_MIX_HEADS)
    head_cols = [slice(h * LANES, (h + 1) * LANES) for h in heads]
    q_h = [q_ref[:, hq * LANES:(hq + 1) * LANES] for hq in range(GDN_QK_HEADS)]
    k_h = [k_ref[:, hq * LANES:(hq + 1) * LANES] for hq in range(GDN_QK_HEADS)]
    k_b = [k.astype(BF16) for k in k_h]
    qk = [_dot_nt(q.astype(BF16), kb16) for q, kb16 in zip(q_h, k_b)]
    g_col = [gcum[:, h:h + 1] for h in heads]
    b_col = [beta[:, h:h + 1] for h in heads]
    decay = [jnp.exp(jnp.where(causal, g_col[h] - gcum_t[h:h + 1, :], -jnp.inf)) for h in heads]
    eg = [jnp.exp(g) for g in g_col]
    kbeta = [k_h[h // rep] * b_col[h] for h in heads]
    kk = [_dot_nt(kbeta[h].astype(BF16), k_b[h // rep]) for h in heads]
    nm = [jnp.where(strict, -(kk[h] * decay[h]), 0.0) for h in heads]
    inv = [eye + n for n in nm]
    pw2 = [_split2(n) for n in nm]
    for _ in range(5):
        pw2 = [_split2(_dot3(p2, p2)) for p2 in pw2]
        inv = [inv[h] + _dot3(pw2[h], _split2(inv[h])) for h in heads]
    for h in heads:
        hd = head_cols[h]
        rhs = jnp.concatenate([v_ref[:, hd] * b_col[h], kbeta[h] * eg[h]], axis=1)
        sol = _dot3(_split2(inv[h]), _split2(rhs))
        u_ref[:, hd] = sol[:, :LANES]
        w_ref[:, hd] = sol[:, LANES:].astype(BF16)
        qg_ref[:, hd] = (q_h[h // rep] * eg[h]).astype(BF16)
        g_last = gcum[c - 1:c, h:h + 1]
        kd_ref[:, hd] = (k_h[h // rep] * jnp.exp(g_last - g_col[h])).astype(BF16)
        aqk_ref[0, h] = (qk[h // rep] * decay[h]).astype(BF16)


def _gdn_chunk(qn, kn, vc, beta, gcum):
    t = qn.shape[0]
    c = GDN_CHUNK
    n = t // c
    row = lambda w: pl.BlockSpec((c, w), lambda i: (i, 0))
    return pl.pallas_call(
        _gdn_chunk_kernel,
        grid=(n,),
        in_specs=[row(GDN_QK_WIDTH), row(GDN_QK_WIDTH), row(MIX_WIDTH), row(LANES), row(LANES)],
        out_specs=[row(MIX_WIDTH), row(MIX_WIDTH), row(MIX_WIDTH), row(MIX_WIDTH),
                   pl.BlockSpec((1, N_MIX_HEADS, c, c), lambda i: (i, 0, 0, 0)),
                   pl.BlockSpec((N_MIX_HEADS, LANES), lambda i: (i, 0))],
        out_shape=[
            jax.ShapeDtypeStruct((t, MIX_WIDTH), F32),
            jax.ShapeDtypeStruct((t, MIX_WIDTH), BF16),
            jax.ShapeDtypeStruct((t, MIX_WIDTH), BF16),
            jax.ShapeDtypeStruct((t, MIX_WIDTH), BF16),
            jax.ShapeDtypeStruct((n, N_MIX_HEADS, c, c), BF16),
            jax.ShapeDtypeStruct((n * N_MIX_HEADS, LANES), F32),
        ],
        compiler_params=_cparams("parallel"),
        name="gdn_chunk",
    )(qn, kn, vc, beta, gcum)


def _gdn_scan_kernel(u_ref, w_ref, qg_ref, kd_ref, aqk_ref, egl_ref, z_ref, on_ref,
                     o_ref, s_ref):
    @pl.when(pl.program_id(0) == 0)
    def _():
        s_ref[...] = jnp.zeros_like(s_ref)

    on = on_ref[...]
    heads = range(N_MIX_HEADS)
    cols = [slice(h * LANES, (h + 1) * LANES) for h in heads]
    s_old = [s_ref[h] for h in heads]
    s_b = [s.astype(BF16) for s in s_old]
    ws = [_dot(w_ref[:, cols[h]], s_b[h]) for h in heads]
    qs = [_dot(qg_ref[:, cols[h]], s_b[h]) for h in heads]
    v_b = [(u_ref[:, cols[h]] - ws[h]).astype(BF16) for h in heads]
    av = [_dot(aqk_ref[0, h], v_b[h]) for h in heads]
    kv = [_dot_tn(kd_ref[:, cols[h]], v_b[h]) for h in heads]
    for h in heads:
        s_ref[h] = s_old[h] * egl_ref[h:h + 1, :] + kv[h]
        o = qs[h] + av[h]
        o = o * lax.rsqrt(jnp.mean(o * o, axis=-1, keepdims=True) + RMS_EPS) * on
        o_ref[:, cols[h]] = o * _silu(z_ref[:, cols[h]])


def _gdn_scan(u, w, qg, kd, aqk, egl, p, onorm):
    t = u.shape[0]
    c = GDN_CHUNK
    n = t // c
    row = lambda wd: pl.BlockSpec((c, wd), lambda i: (i, 0))
    return pl.pallas_call(
        _gdn_scan_kernel,
        grid=(n,),
        in_specs=[row(MIX_WIDTH), row(MIX_WIDTH), row(MIX_WIDTH), row(MIX_WIDTH),
                  pl.BlockSpec((1, N_MIX_HEADS, c, c), lambda i: (i, 0, 0, 0)),
                  pl.BlockSpec((N_MIX_HEADS, LANES), lambda i: (i, 0)),
                  pl.BlockSpec((c, MIX_WIDTH), lambda i: (i, GDN_COL_Z // MIX_WIDTH)),
                  pl.BlockSpec((1, LANES), lambda i: (0, 0))],
        out_specs=row(MIX_WIDTH),
        out_shape=jax.ShapeDtypeStruct((t, MIX_WIDTH), F32),
        scratch_shapes=[pltpu.VMEM((N_MIX_HEADS, HEAD_DIM, HEAD_DIM), F32)],
        compiler_params=_cparams("arbitrary"),
        name="gdn_scan",
    )(u, w, qg, kd, aqk, egl, p, onorm)


SB_SUBS = 4


def _sb_tiles(q_subs, k_ref, v_ref, kbs, c_runs, tri, keeps):
    blk = SB_BLOCK
    subs = range(len(q_subs))
    starts = [pl.multiple_of(kb * blk, blk) for kb in kbs]
    zs = [_dot_nt(q_subs[r], k_ref[pl.ds(starts[r], blk), :].astype(BF16)) for r in subs]
    ts = [jnp.log1p(jnp.exp(-jnp.abs(z))) for z in zs]
    log_not = [jnp.where(keeps[r], -jnp.maximum(zs[r], 0.0) - ts[r], 0.0) for r in subs]
    log_beta = [jnp.minimum(zs[r], 0.0) - ts[r] for r in subs]
    hi = [x.astype(BF16) for x in log_not]
    rest = [log_not[r] - hi[r].astype(F32) for r in subs]
    mid = [x.astype(BF16) for x in rest]
    lo = [(rest[r] - mid[r].astype(F32)).astype(BF16) for r in subs]
    suffix = [_dot(hi[r], tri) + (_dot(mid[r], tri) + _dot(lo[r], tri)) for r in subs]
    att = [jnp.where(keeps[r], jnp.exp(log_beta[r] + suffix[r] + c_runs[r]), 0.0).astype(BF16) for r in subs]
    contrib = [_dot(att[r], v_ref[pl.ds(starts[r], blk), :].astype(BF16)) for r in subs]
    c_new = [c_runs[r] + jnp.sum(log_not[r], axis=-1, keepdims=True) for r in subs]
    return c_new, contrib


def _sb_kernel(q_ref, k_ref, v_ref, o_ref):
    blk, subs = SB_BLOCK, SB_SUBS
    qb = pl.program_id(1)
    ri = lax.broadcasted_iota(jnp.int32, (blk, blk), 0)
    ci = lax.broadcasted_iota(jnp.int32, (blk, blk), 1)
    tri = jnp.where(ri > ci, 1.0, 0.0).astype(BF16)
    q_subs = [(q_ref[r * blk:(r + 1) * blk, :] * (HEAD_DIM ** -0.5)).astype(BF16) for r in range(subs)]

    def all_max(cs):
        m = cs[0]
        for c in cs[1:]:
            m = jnp.maximum(m, c)
        return jnp.max(m)

    cs, contrib = _sb_tiles(q_subs, k_ref, v_ref, [qb * subs + r for r in range(subs)],
                            [jnp.zeros((blk, 1), F32)] * subs, tri, [ci < ri] * subs)
    for r in range(subs):
        o_ref[r * blk:(r + 1) * blk, :] = contrib[r]

    last = qb * subs + subs - 1

    def cond(carry):
        return jnp.logical_and(carry[0] <= last, carry[1])

    def body(carry):
        n = carry[0]
        kbs = [qb * subs + r - n for r in range(subs)]
        new, contrib = _sb_tiles(q_subs, k_ref, v_ref, [jnp.maximum(kb, 0) for kb in kbs],
                                 list(carry[2:]), tri, [kb >= 0 for kb in kbs])
        for r in range(subs):
            o_ref[r * blk:(r + 1) * blk, :] += contrib[r]
        return (n + 1, all_max(new) > EXP_ZERO_BELOW, *new)

    lax.while_loop(cond, body, (jnp.int32(1), all_max(cs) > EXP_ZERO_BELOW, *cs))


def _stick_breaking(p):
    t = p.shape[0]
    blk = SB_BLOCK * SB_SUBS
    return pl.pallas_call(
        _sb_kernel,
        grid=(N_MIX_HEADS, t // blk),
        in_specs=[pl.BlockSpec((blk, HEAD_DIM), lambda h, i: (i, h)),
                  pl.BlockSpec((t, HEAD_DIM), lambda h, i: (0, N_MIX_HEADS + h)),
                  pl.BlockSpec((t, HEAD_DIM), lambda h, i: (0, 2 * N_MIX_HEADS + h))],
        out_specs=pl.BlockSpec((blk, HEAD_DIM), lambda h, i: (i, h)),
        out_shape=jax.ShapeDtypeStruct((t, MIX_WIDTH), F32),
        compiler_params=_cparams("parallel", "parallel"),
        name="stick_breaking",
    )(p, p, p)


MEM_ROWS = 512


def _mem_attn_kernel(q_ref, kv_ref, o_ref):
    for h in range(MEM_HEADS):
        hd = slice(h * LANES, (h + 1) * LANES)
        km = kv_ref[:, hd].astype(BF16)
        vm = kv_ref[:, MEM_WIDTH + h * LANES:MEM_WIDTH + (h + 1) * LANES].astype(BF16)
        s = _dot_nt(q_ref[:, hd].astype(BF16), km) * (HEAD_DIM ** -0.5)
        s = s - jnp.max(s, axis=-1, keepdims=True)
        e = jnp.exp(s)
        prob = e / jnp.sum(e, axis=-1, keepdims=True)
        o_ref[:, hd] = _dot(prob.astype(BF16), vm)


def _mem_attn(p, kv, qm_col):
    t = p.shape[0]
    tb = MEM_ROWS
    return pl.pallas_call(
        _mem_attn_kernel,
        grid=(t // tb,),
        in_specs=[pl.BlockSpec((tb, MEM_WIDTH), lambda i: (i, qm_col // MEM_WIDTH)),
                  pl.BlockSpec((N_MEM, 2 * MEM_WIDTH), lambda i: (0, 0))],
        out_specs=pl.BlockSpec((tb, MEM_WIDTH), lambda i: (i, 0)),
        out_shape=jax.ShapeDtypeStruct((t, MEM_WIDTH), F32),
        compiler_params=_cparams("parallel"),
        name="mem_attn",
    )(p, kv)


OUT_ROWS = 256


def _out_ln_kernel(mix_ref, mem_ref, x_ref, w_ref, g_ref, b_ref, o_ref, ot_ref):
    y = _dot(mix_ref[...].astype(BF16), w_ref[0:MIX_WIDTH, :])
    y = y + _dot(mem_ref[...].astype(BF16), w_ref[MIX_WIDTH:, :])
    o = _layer_norm(DN_ALPHA * x_ref[...] + y, g_ref[...], b_ref[...])
    o_ref[...] = o
    ot_ref[...] = o.T.astype(BF16)


def _out_ln(mix, mem_o, x, w_out, g, b):
    t, d = x.shape
    tb = OUT_ROWS
    return pl.pallas_call(
        _out_ln_kernel,
        grid=(t // tb,),
        in_specs=[pl.BlockSpec((tb, MIX_WIDTH), lambda i: (i, 0)),
                  pl.BlockSpec((tb, MEM_WIDTH), lambda i: (i, 0)),
                  pl.BlockSpec((tb, d), lambda i: (i, 0)),
                  pl.BlockSpec((MIX_WIDTH + MEM_WIDTH, d), lambda i: (0, 0)),
                  pl.BlockSpec((1, d), lambda i: (0, 0)),
                  pl.BlockSpec((1, d), lambda i: (0, 0))],
        out_specs=[pl.BlockSpec((tb, d), lambda i: (i, 0)),
                   pl.BlockSpec((d, tb), lambda i: (0, i))],
        out_shape=[jax.ShapeDtypeStruct((t, d), F32),
                   jax.ShapeDtypeStruct((d, t), BF16)],
        compiler_params=_cparams("parallel"),
        name="out_proj_ln",
    )(mix, mem_o, x, w_out, g, b)


ROUTE_COLS = 256
PEER_NCAND = PEER_HALF_TOPK * PEER_HALF_TOPK
PEER_SCORE_ROWS = PEER_HEADS * 2 * PEER_NKEYS
AUX_ROWS = 16
AUX_THR = 0
AUX_OFF = PEER_HEADS
AUX_TIE = 2 * PEER_HEADS


def _top_rows(vals, row_idx, count, limit, emit):
    for r in range(count):
        m = jnp.max(vals, axis=0, keepdims=True)
        am = jnp.min(jnp.where(vals == m, row_idx, float(limit)), axis=0, keepdims=True)
        sel = row_idx == am
        emit(r, m, am, sel)
        vals = jnp.where(sel, -jnp.inf, vals)
    return jnp.max(vals, axis=0, keepdims=True)


def _route_kernel(xt_ref, wq_ref, keys_ref, sc_ref, aux_ref, ids_ref, gates_ref,
                  sv_ref, ix_ref, cand_ref, eid_ref):
    tb = ROUTE_COLS
    qry = _dot(wq_ref[...], xt_ref[...])
    row_k = lax.broadcasted_iota(jnp.int32, (PEER_NKEYS, tb), 0).astype(F32)
    row_c = lax.broadcasted_iota(jnp.int32, (PEER_NCAND, tb), 0).astype(F32)
    tie = jnp.zeros((1, tb), F32)
    for h in range(PEER_HEADS):
        left = []
        for half in range(2):
            r0 = (h * 2 + half) * PEER_NKEYS
            s = _dot(keys_ref[half], qry[r0:r0 + PEER_NKEYS, :].astype(BF16))
            sc_ref[r0:r0 + PEER_NKEYS, :] = s

            def emit_half(r, m, am, sel, half=half):
                sv_ref[half, r:r + 1, :] = m
                ix_ref[half, r:r + 1, :] = am

            left.append(_top_rows(s, row_k, PEER_HALF_TOPK, PEER_NKEYS, emit_half))
        sv1, sv2 = sv_ref[0], sv_ref[1]
        ix1, ix2 = ix_ref[0], ix_ref[1]
        for a in range(PEER_HALF_TOPK):
            rows = slice(a * PEER_HALF_TOPK, (a + 1) * PEER_HALF_TOPK)
            cand_ref[rows, :] = sv1[a:a + 1, :] + sv2
            eid_ref[rows, :] = ix1[a:a + 1, :] * float(PEER_NKEYS) + ix2
        eid = eid_ref[...]
        cvs, eids = [], []

        def emit_cand(r, m, am, sel):
            cvs.append(m)
            eids.append(jnp.max(jnp.where(sel, eid, -1.0), axis=0, keepdims=True))

        runner_up = _top_rows(cand_ref[...], row_c, PEER_TOPK, PEER_NCAND, emit_cand)
        thr = cvs[PEER_TOPK - 1]
        outside = jnp.maximum(left[0] + sv2[0:1, :], sv1[0:1, :] + left[1])
        tie = jnp.maximum(tie, jnp.where(jnp.maximum(outside, runner_up) >= thr, 1.0, 0.0))
        exps = [jnp.exp(cv - cvs[0]) for cv in cvs]
        denom = exps[0]
        for e in exps[1:]:
            denom = denom + e
        aux_ref[AUX_THR + h:AUX_THR + h + 1, :] = thr
        aux_ref[AUX_OFF + h:AUX_OFF + h + 1, :] = -(cvs[0] + jnp.log(denom))
        for r in range(PEER_TOPK):
            slot = h * PEER_TOPK + r
            ids_ref[slot:slot + 1, :] = eids[r].astype(jnp.int32)
            gates_ref[slot:slot + 1, :] = exps[r] / denom
    aux_ref[AUX_TIE:AUX_TIE + 1, :] = tie
    aux_ref[AUX_TIE + 1:, :] = jnp.zeros((AUX_ROWS - AUX_TIE - 1, tb), F32)


def _route(xt, wq_t, keys):
    d, t = xt.shape
    tb = ROUTE_COLS
    col = lambda rows: pl.BlockSpec((rows, tb), lambda i: (0, i))
    return pl.pallas_call(
        _route_kernel,
        grid=(t // tb,),
        in_specs=[col(d),
                  pl.BlockSpec((PEER_SCORE_ROWS, d), lambda i: (0, 0)),
                  pl.BlockSpec((2, PEER_NKEYS, PEER_HALF), lambda i: (0, 0, 0))],
        out_specs=[col(PEER_SCORE_ROWS), col(AUX_ROWS), col(PEER_SLOTS), col(PEER_SLOTS)],
        out_shape=[jax.ShapeDtypeStruct((PEER_SCORE_ROWS, t), F32),
                   jax.ShapeDtypeStruct((AUX_ROWS, t), F32),
                   jax.ShapeDtypeStruct((PEER_SLOTS, t), jnp.int32),
                   jax.ShapeDtypeStruct((PEER_SLOTS, t), F32)],
        scratch_shapes=[pltpu.VMEM((2, PEER_HALF_TOPK, tb), F32),
                        pltpu.VMEM((2, PEER_HALF_TOPK, tb), F32),
                        pltpu.VMEM((PEER_NCAND, tb), F32),
                        pltpu.VMEM((PEER_NCAND, tb), F32)],
        compiler_params=_cparams("parallel"),
        name="peer_route",
    )(xt, wq_t, keys)


PEER_COLS = 512
PEER_ECHUNK = 1024


def _peer_kernel(xt_ref, x_ref, sc_ref, aux_ref, ids_ref, gates_ref, u_ref, vt_ref, g_ref, b_ref,
                 o_ref, acc_ref):
    j = pl.program_id(1)
    tb, ec = PEER_COLS, PEER_ECHUNK
    groups = ec // PEER_NKEYS

    @pl.when(j == 0)
    def _():
        acc_ref[...] = jnp.zeros_like(acc_ref)

    def threshold_gates(gi):
        w = jnp.zeros((PEER_NKEYS, tb), F32)
        for h in range(PEER_HEADS):
            base = h * 2 * PEER_NKEYS
            s1 = sc_ref[pl.ds(base + j * groups + gi, 1), :]
            s2 = sc_ref[base + PEER_NKEYS:base + 2 * PEER_NKEYS, :]
            dsum = s1 + s2
            gate = jnp.exp(dsum + aux_ref[AUX_OFF + h:AUX_OFF + h + 1, :])
            w = w + jnp.where(dsum >= aux_ref[AUX_THR + h:AUX_THR + h + 1, :], gate, 0.0)
        return w

    def scattered_gates(gi):
        row_e = lax.broadcasted_iota(jnp.int32, (PEER_NKEYS, tb), 0) + (j * ec + gi * PEER_NKEYS)
        w = jnp.zeros((PEER_NKEYS, tb), F32)
        for s in range(PEER_SLOTS):
            w = w + jnp.where(ids_ref[s:s + 1, :] == row_e, gates_ref[s:s + 1, :], 0.0)
        return w

    def step(gates_of):
        xt = xt_ref[...]
        acts = []
        for gi in range(groups):
            rows = slice(gi * PEER_NKEYS, (gi + 1) * PEER_NKEYS)
            hid = _dot(u_ref[rows, :], xt)
            gelu = 0.5 * hid * (1.0 + lax.erf(hid * (2.0 ** -0.5)))
            acts.append((gelu * gates_of(gi)).astype(BF16))
        acc_ref[...] += _dot(vt_ref[...], jnp.concatenate(acts, axis=0))

    any_tie = jnp.max(aux_ref[AUX_TIE:AUX_TIE + 1, :]) > 0.0
    pl.when(jnp.logical_not(any_tie))(lambda: step(threshold_gates))
    pl.when(any_tie)(lambda: step(scattered_gates))

    @pl.when(j == pl.num_programs(1) - 1)
    def _():
        o_ref[...] = _layer_norm(DN_ALPHA * x_ref[...] + acc_ref[...].T, g_ref[...], b_ref[...])


def _peer(xt, x, scores, aux, ids, gates, u, vt, g, b):
    t, d = x.shape
    tb, ec = PEER_COLS, PEER_ECHUNK
    ne = u.shape[0]
    col = lambda rows: pl.BlockSpec((rows, tb), lambda i, j: (0, i))
    return pl.pallas_call(
        _peer_kernel,
        grid=(t // tb, ne // ec),
        in_specs=[col(d),
                  pl.BlockSpec((tb, d), lambda i, j: (i, 0)),
                  col(PEER_SCORE_ROWS), col(AUX_ROWS), col(PEER_SLOTS), col(PEER_SLOTS),
                  pl.BlockSpec((ec, d), lambda i, j: (j, 0)),
                  pl.BlockSpec((d, ec), lambda i, j: (0, j)),
                  pl.BlockSpec((1, d), lambda i, j: (0, 0)),
                  pl.BlockSpec((1, d), lambda i, j: (0, 0))],
        out_specs=pl.BlockSpec((tb, d), lambda i, j: (i, 0)),
        out_shape=jax.ShapeDtypeStruct((t, d), F32),
        scratch_shapes=[pltpu.VMEM((d, tb), F32)],
        compiler_params=_cparams("parallel", "arbitrary"),
        name="peer_dense",
    )(xt, x, scores, aux, ids, gates, u, vt, g, b)


def _pad_lanes(vec):
    return jnp.zeros((1, LANES), F32).at[0, :vec.shape[0]].set(vec.astype(F32))


def _gdn_weight(w):
    qw, vw, nh = GDN_QK_WIDTH, MIX_WIDTH, N_MIX_HEADS
    d = w.shape[0]
    o_b = 2 * qw + 2 * vw
    main = w[:, :o_b]
    b_cols = w[:, o_b:o_b + nh]
    a_cols = w[:, o_b + nh:o_b + 2 * nh]
    qm = w[:, o_b + 2 * nh:]
    zpad = jnp.zeros((d, LANES - nh), w.dtype)
    return jnp.concatenate([main, qm, b_cols, zpad, a_cols, zpad], axis=1).astype(BF16)


def _forward(x, mem, gdn_w_in, gdn_conv, gdn_a_log, gdn_dt_bias, gdn_onorm, sb_w_in, mem_w_kv, w_out,
             ln_mix_g, ln_mix_b, peer_w_q, peer_keys, peer_u, peer_v, ln_ffn_g, ln_ffn_b):
    t = x.shape[1]
    d = x.shape[2]
    xt = x.reshape(t, d)
    memt = mem.reshape(N_MEM, d)
    row = lambda a: a.reshape(1, -1).astype(F32)
    for i in range(DEPTH):
        li = i // 2
        if i % 2 == 0:
            p = _matmul(xt, _gdn_weight(gdn_w_in[li]), 512, 768)
            qn, kn, vc, beta, gcum = _gdn_prep(p, gdn_conv[li].astype(F32),
                                               _pad_lanes(gdn_a_log[li]), _pad_lanes(gdn_dt_bias[li]))
            u, w, qg, kd, aqk, egl = _gdn_chunk(qn, kn, vc, beta, gcum)
            mix = _gdn_scan(u, w, qg, kd, aqk, egl, p, row(gdn_onorm[li]))
            qm_col = GDN_COL_QM
        else:
            p = _matmul(xt, sb_w_in[li].astype(BF16), 512, 512)
            mix = _stick_breaking(p)
            qm_col = SB_COL_QM
        kv = _matmul(memt, mem_w_kv[i].astype(BF16), N_MEM, 512)
        mem_o = _mem_attn(p, kv, qm_col)
        x1, x1_t = _out_ln(mix, mem_o, xt, w_out[i].astype(BF16), row(ln_mix_g[i]), row(ln_mix_b[i]))
        scores, aux, ids, gates = _route(x1_t, peer_w_q[i].T.astype(BF16), peer_keys[i].astype(BF16))
        xt = _peer(x1_t, x1, scores, aux, ids, gates, peer_u[i].astype(BF16), peer_v[i].T.astype(BF16),
                   row(ln_ffn_g[i]), row(ln_ffn_b[i]))
    return xt.reshape(x.shape)


def kernel(x, mem, gdn_w_in, gdn_conv, gdn_a_log, gdn_dt_bias, gdn_onorm, sb_w_in, mem_w_kv, w_out,
           ln_mix_g, ln_mix_b, peer_w_q, peer_keys, peer_u, peer_v, ln_ffn_g, ln_ffn_b):
    return _forward(x, mem, gdn_w_in, gdn_conv, gdn_a_log, gdn_dt_bias, gdn_onorm, sb_w_in, mem_w_kv,
                    w_out, ln_mix_g, ln_mix_b, peer_w_q, peer_keys, peer_u, peer_v, ln_ffn_g, ln_ffn_b)
```

```python
import functools
import math

import jax
import jax.numpy as jnp
from jax import lax
from jax.experimental import pallas as pl
from jax.experimental.pallas import tpu as pltpu

F32 = jnp.float32
BF16 = jnp.bfloat16
HI = lax.Precision.HIGHEST

D_MODEL = 2048
DEPTH = 4
HEAD_DIM = 128
N_MIX_HEADS = 8
GDN_QK_HEADS = 4
MIX_WIDTH = N_MIX_HEADS * HEAD_DIM
GDN_QK_WIDTH = GDN_QK_HEADS * HEAD_DIM
N_MEM = 256
MEM_HEADS = 4
MEM_WIDTH = MEM_HEADS * HEAD_DIM
CONV_WIDTH = 4
GDN_CHUNK = 64
SB_BLOCK = 128
PEER_HEADS = 4
PEER_NKEYS = 128
PEER_EXPERTS = PEER_NKEYS * PEER_NKEYS
PEER_HALF = 128
PEER_HALF_TOPK = 16
PEER_TOPK = 8
PEER_SLOTS = PEER_HEADS * PEER_TOPK
DN_ALPHA = (2.0 * DEPTH) ** 0.25
LN_EPS = 1e-5
RMS_EPS = 1e-6
LANES = 128
SUBLANES = 8
EXP_ZERO_BELOW = -104.0

GDN_P_WIDTH = 3840
GDN_COL_Z = 2048
GDN_COL_QM = 3072
GDN_COL_B = 3584
GDN_COL_A = 3712
SB_P_WIDTH = 3 * MIX_WIDTH + MEM_WIDTH
SB_COL_QM = 3 * MIX_WIDTH

VMEM_LIMIT = 56 * 1024 * 1024


def _cparams(*sem):
    return pltpu.CompilerParams(dimension_semantics=sem, vmem_limit_bytes=VMEM_LIMIT)


def _dot(a, b):
    return jnp.dot(a, b, preferred_element_type=F32)


def _dot_nt(a, b):
    return lax.dot_general(a, b, (((1,), (1,)), ((), ())), preferred_element_type=F32)


def _dot_tn(a, b):
    return lax.dot_general(a, b, (((0,), (0,)), ((), ())), preferred_element_type=F32)


def _dot_hi(a, b):
    return jnp.dot(a, b, precision=HI, preferred_element_type=F32)


def _split2(a):
    hi = a.astype(BF16)
    lo = (a - hi.astype(F32)).astype(BF16)
    return hi, lo


def _dot3(a2, b2):
    return _dot(a2[0], b2[0]) + (_dot(a2[0], b2[1]) + _dot(a2[1], b2[0]))


def _softplus(x):
    return jnp.maximum(x, 0.0) + jnp.log1p(jnp.exp(-jnp.abs(x)))


def _silu(x):
    return x * jax.nn.sigmoid(x)


def _layer_norm(r, g, b):
    mu = jnp.mean(r, axis=-1, keepdims=True)
    d = r - mu
    var = jnp.mean(d * d, axis=-1, keepdims=True)
    return d * lax.rsqrt(var + LN_EPS) * g + b


def _mm_kernel(x_ref, w_ref, o_ref):
    o_ref[...] = _dot(x_ref[...].astype(BF16), w_ref[...])


def _matmul(x, w, tm, tn):
    m, k = x.shape
    n = w.shape[1]
    return pl.pallas_call(
        _mm_kernel,
        grid=(m // tm, n // tn),
        in_specs=[pl.BlockSpec((tm, k), lambda i, j: (i, 0)),
                  pl.BlockSpec((k, tn), lambda i, j: (0, j))],
        out_specs=pl.BlockSpec((tm, tn), lambda i, j: (i, j)),
        out_shape=jax.ShapeDtypeStruct((m, n), F32),
        compiler_params=_cparams("parallel", "parallel"),
        name="proj_matmul",
    )(x, w)


GDN_PREP_ROWS = 256


def _gdn_prep_kernel(pc_ref, ph_ref, pb_ref, pa_ref, cw_ref, alog_ref, dtb_ref,
                     q_ref, k_ref, v_ref, beta_ref, gcum_ref, xbuf):
    tb = GDN_PREP_ROWS
    first = pl.program_id(0) == 0
    halo = ph_ref[...]
    xbuf[0:SUBLANES, :] = jnp.where(first, 0.0, halo)
    xbuf[SUBLANES:, :] = pc_ref[...]
    n_groups = (2 * GDN_QK_WIDTH + MIX_WIDTH) // LANES
    for gi in range(n_groups):
        cols = slice(gi * LANES, (gi + 1) * LANES)
        y = jnp.zeros((tb, LANES), F32)
        for j in range(CONV_WIDTH):
            r0 = SUBLANES - (CONV_WIDTH - 1) + j
            tap = xbuf[r0:r0 + tb, cols]
            y = y + tap * cw_ref[j:j + 1, cols]
        y = _silu(y)
        if gi < 2 * GDN_QK_HEADS:
            y = y * lax.rsqrt(jnp.sum(y * y, axis=-1, keepdims=True) + RMS_EPS)
        if gi < GDN_QK_HEADS:
            q_ref[:, cols] = y * (HEAD_DIM ** -0.5)
        elif gi < 2 * GDN_QK_HEADS:
            k_ref[:, (gi - GDN_QK_HEADS) * LANES:(gi - GDN_QK_HEADS + 1) * LANES] = y
        else:
            v_ref[:, (gi - 2 * GDN_QK_HEADS) * LANES:(gi - 2 * GDN_QK_HEADS + 1) * LANES] = y
    beta_ref[...] = jax.nn.sigmoid(pb_ref[...])
    g = -jnp.exp(alog_ref[...]) * _softplus(pa_ref[...] + dtb_ref[...])
    r = lax.broadcasted_iota(jnp.int32, (tb, tb), 0)
    c = lax.broadcasted_iota(jnp.int32, (tb, tb), 1)
    shift = GDN_CHUNK.bit_length() - 1
    same = lax.shift_right_logical(r, shift) == lax.shift_right_logical(c, shift)
    tri = jnp.where(same & (c <= r), 1.0, 0.0).astype(F32)
    gcum_ref[...] = _dot_hi(tri, g)


def _gdn_prep(p, conv_w, alog_pad, dtb_pad):
    t = p.shape[0]
    tb = GDN_PREP_ROWS
    cw = 2 * GDN_QK_WIDTH + MIX_WIDTH
    per8 = tb // SUBLANES
    return pl.pallas_call(
        _gdn_prep_kernel,
        grid=(t // tb,),
        in_specs=[
            pl.BlockSpec((tb, cw), lambda i: (i, 0)),
            pl.BlockSpec((SUBLANES, cw), lambda i: (jnp.maximum(i * per8 - 1, 0), 0)),
            pl.BlockSpec((tb, LANES), lambda i: (i, GDN_COL_B // LANES)),
            pl.BlockSpec((tb, LANES), lambda i: (i, GDN_COL_A // LANES)),
            pl.BlockSpec((CONV_WIDTH, cw), lambda i: (0, 0)),
            pl.BlockSpec((1, LANES), lambda i: (0, 0)),
            pl.BlockSpec((1, LANES), lambda i: (0, 0)),
        ],
        out_specs=[
            pl.BlockSpec((tb, GDN_QK_WIDTH), lambda i: (i, 0)),
            pl.BlockSpec((tb, GDN_QK_WIDTH), lambda i: (i, 0)),
            pl.BlockSpec((tb, MIX_WIDTH), lambda i: (i, 0)),
            pl.BlockSpec((tb, LANES), lambda i: (i, 0)),
            pl.BlockSpec((tb, LANES), lambda i: (i, 0)),
        ],
        out_shape=[
            jax.ShapeDtypeStruct((t, GDN_QK_WIDTH), F32),
            jax.ShapeDtypeStruct((t, GDN_QK_WIDTH), F32),
            jax.ShapeDtypeStruct((t, MIX_WIDTH), F32),
            jax.ShapeDtypeStruct((t, LANES), F32),
            jax.ShapeDtypeStruct((t, LANES), F32),
        ],
        scratch_shapes=[pltpu.VMEM((tb + SUBLANES, cw), F32)],
        compiler_params=_cparams("parallel"),
        name="gdn_prep",
    )(p, p, p, p, conv_w, alog_pad, dtb_pad)


def _gdn_chunk_kernel(q_ref, k_ref, v_ref, beta_ref, gcum_ref,
                      u_ref, w_ref, qg_ref, kd_ref, aqk_ref, egl_ref):
    c = GDN_CHUNK
    beta = beta_ref[...]
    gcum = gcum_ref[...]
    gcum_t = gcum.T
    glast_col = gcum_t[:, c - 1:c]
    egl_ref[...] = jnp.exp(jnp.broadcast_to(glast_col, (LANES, LANES))[0:N_MIX_HEADS, :])
    ri = lax.broadcasted_iota(jnp.int32, (c, c), 0)
    ci = lax.broadcasted_iota(jnp.int32, (c, c), 1)
    causal = ci <= ri
    strict = ci < ri
    eye = jnp.where(ci == ri, 1.0, 0.0).astype(F32)
    rep = N_MIX_HEADS // GDN_QK_HEADS
    heads = range(N_MIX_HEADS)
    head_cols = [slice(h * LANES, (h + 1) * LANES) for h in heads]
    q_h = [q_ref[:, hq * LANES:(hq + 1) * LANES] for hq in range(GDN_QK_HEADS)]
    k_h = [k_ref[:, hq * LANES:(hq + 1) * LANES] for hq in range(GDN_QK_HEADS)]
    k_b = [k.astype(BF16) for k in k_h]
    qk = [_dot_nt(q.astype(BF16), kb16) for q, kb16 in zip(q_h, k_b)]
    g_col = [gcum[:, h:h + 1] for h in heads]
    b_col = [beta[:, h:h + 1] for h in heads]
    decay = [jnp.exp(jnp.where(causal, g_col[h] - gcum_t[h:h + 1, :], -jnp.inf)) for h in heads]
    eg = [jnp.exp(g) for g in g_col]
    kbeta = [k_h[h // rep] * b_col[h] for h in heads]
    kk = [_dot_nt(kbeta[h].astype(BF16), k_b[h // rep]) for h in heads]
    nm = [jnp.where(strict, -(kk[h] * decay[h]), 0.0) for h in heads]
    inv = [eye + n for n in nm]
    pw2 = [_split2(n) for n in nm]
    for _ in range(5):
        pw2 = [_split2(_dot3(p2, p2)) for p2 in pw2]
        inv = [inv[h] + _dot3(pw2[h], _split2(inv[h])) for h in heads]
    for h in heads:
        hd = head_cols[h]
        rhs = jnp.concatenate([v_ref[:, hd] * b_col[h], kbeta[h] * eg[h]], axis=1)
        sol = _dot3(_split2(inv[h]), _split2(rhs))
        u_ref[:, hd] = sol[:, :LANES]
        w_ref[:, hd] = sol[:, LANES:].astype(BF16)
        qg_ref[:, hd] = (q_h[h // rep] * eg[h]).astype(BF16)
        g_last = gcum[c - 1:c, h:h + 1]
        kd_ref[:, hd] = (k_h[h // rep] * jnp.exp(g_last - g_col[h])).astype(BF16)
        aqk_ref[0, h] = (qk[h // rep] * decay[h]).astype(BF16)


def _gdn_chunk(qn, kn, vc, beta, gcum):
    t = qn.shape[0]
    c = GDN_CHUNK
    n = t // c
    row = lambda w: pl.BlockSpec((c, w), lambda i: (i, 0))
    return pl.pallas_call(
        _gdn_chunk_kernel,
        grid=(n,),
        in_specs=[row(GDN_QK_WIDTH), row(GDN_QK_WIDTH), row(MIX_WIDTH), row(LANES), row(LANES)],
        out_specs=[row(MIX_WIDTH), row(MIX_WIDTH), row(MIX_WIDTH), row(MIX_WIDTH),
                   pl.BlockSpec((1, N_MIX_HEADS, c, c), lambda i: (i, 0, 0, 0)),
                   pl.BlockSpec((N_MIX_HEADS, LANES), lambda i: (i, 0))],
        out_shape=[
            jax.ShapeDtypeStruct((t, MIX_WIDTH), F32),
            jax.ShapeDtypeStruct((t, MIX_WIDTH), BF16),
            jax.ShapeDtypeStruct((t, MIX_WIDTH), BF16),
            jax.ShapeDtypeStruct((t, MIX_WIDTH), BF16),
            jax.ShapeDtypeStruct((n, N_MIX_HEADS, c, c), BF16),
            jax.ShapeDtypeStruct((n * N_MIX_HEADS, LANES), F32),
        ],
        compiler_params=_cparams("parallel"),
        name="gdn_chunk",
    )(qn, kn, vc, beta, gcum)


GDN_SCAN_CHUNKS = 2


def _gdn_scan_kernel(u_ref, w_ref, qg_ref, kd_ref, aqk_ref, egl_ref, z_ref, on_ref,
                     o_ref, s_ref):
    @pl.when(pl.program_id(0) == 0)
    def _():
        s_ref[...] = jnp.zeros_like(s_ref)

    on = on_ref[...]
    c = GDN_CHUNK
    heads = range(N_MIX_HEADS)
    cols = [slice(h * LANES, (h + 1) * LANES) for h in heads]
    state = [s_ref[h] for h in heads]
    for ck in range(GDN_SCAN_CHUNKS):
        rows = slice(ck * c, (ck + 1) * c)
        s_b = [s.astype(BF16) for s in state]
        ws = [_dot(w_ref[rows, cols[h]], s_b[h]) for h in heads]
        qs = [_dot(qg_ref[rows, cols[h]], s_b[h]) for h in heads]
        v_b = [(u_ref[rows, cols[h]] - ws[h]).astype(BF16) for h in heads]
        av = [_dot(aqk_ref[ck, h], v_b[h]) for h in heads]
        kv = [_dot_tn(kd_ref[rows, cols[h]], v_b[h]) for h in heads]
        egl = egl_ref[ck * N_MIX_HEADS:(ck + 1) * N_MIX_HEADS, :]
        state = [state[h] * egl[h:h + 1, :] + kv[h] for h in heads]
        for h in heads:
            o = qs[h] + av[h]
            o = o * lax.rsqrt(jnp.mean(o * o, axis=-1, keepdims=True) + RMS_EPS) * on
            o_ref[rows, cols[h]] = o * _silu(z_ref[rows, cols[h]])
    for h in heads:
        s_ref[h] = state[h]


def _gdn_scan(u, w, qg, kd, aqk, egl, p, onorm):
    t = u.shape[0]
    nck = GDN_SCAN_CHUNKS
    c = GDN_CHUNK * nck
    n = t // c
    row = lambda wd: pl.BlockSpec((c, wd), lambda i: (i, 0))
    return pl.pallas_call(
        _gdn_scan_kernel,
        grid=(n,),
        in_specs=[row(MIX_WIDTH), row(MIX_WIDTH), row(MIX_WIDTH), row(MIX_WIDTH),
                  pl.BlockSpec((nck, N_MIX_HEADS, GDN_CHUNK, GDN_CHUNK), lambda i: (i, 0, 0, 0)),
                  pl.BlockSpec((nck * N_MIX_HEADS, LANES), lambda i: (i, 0)),
                  pl.BlockSpec((c, MIX_WIDTH), lambda i: (i, GDN_COL_Z // MIX_WIDTH)),
                  pl.BlockSpec((1, LANES), lambda i: (0, 0))],
        out_specs=row(MIX_WIDTH),
        out_shape=jax.ShapeDtypeStruct((t, MIX_WIDTH), F32),
        scratch_shapes=[pltpu.VMEM((N_MIX_HEADS, HEAD_DIM, HEAD_DIM), F32)],
        compiler_params=_cparams("arbitrary"),
        name="gdn_scan",
    )(u, w, qg, kd, aqk, egl, p, onorm)


SB_SUBS = 8


def _sb_tiles(q_subs, k_ref, v_ref, kbs, c_runs, tri, keeps):
    blk = SB_BLOCK
    subs = range(len(q_subs))
    starts = [pl.multiple_of(kb * blk, blk) for kb in kbs]
    zs = [_dot_nt(q_subs[r], k_ref[pl.ds(starts[r], blk), :].astype(BF16)) for r in subs]
    ts = [jnp.log1p(jnp.exp(-jnp.abs(z))) for z in zs]
    log_not = [jnp.where(keeps[r], -jnp.maximum(zs[r], 0.0) - ts[r], 0.0) for r in subs]
    log_beta = [jnp.minimum(zs[r], 0.0) - ts[r] for r in subs]
    hi = [x.astype(BF16) for x in log_not]
    rest = [log_not[r] - hi[r].astype(F32) for r in subs]
    mid = [x.astype(BF16) for x in rest]
    lo = [(rest[r] - mid[r].astype(F32)).astype(BF16) for r in subs]
    suffix = [_dot(hi[r], tri) + (_dot(mid[r], tri) + _dot(lo[r], tri)) for r in subs]
    att = [jnp.where(keeps[r], jnp.exp(log_beta[r] + suffix[r] + c_runs[r]), 0.0).astype(BF16) for r in subs]
    contrib = [_dot(att[r], v_ref[pl.ds(starts[r], blk), :].astype(BF16)) for r in subs]
    c_new = [c_runs[r] + jnp.sum(log_not[r], axis=-1, keepdims=True) for r in subs]
    return c_new, contrib


def _sb_kernel(q_ref, k_ref, v_ref, o_ref):
    blk, subs = SB_BLOCK, SB_SUBS
    qb = pl.program_id(1)
    ri = lax.broadcasted_iota(jnp.int32, (blk, blk), 0)
    ci = lax.broadcasted_iota(jnp.int32, (blk, blk), 1)
    tri = jnp.where(ri > ci, 1.0, 0.0).astype(BF16)
    q_subs = [(q_ref[r * blk:(r + 1) * blk, :] * (HEAD_DIM ** -0.5)).astype(BF16) for r in range(subs)]

    def all_max(cs):
        m = cs[0]
        for c in cs[1:]:
            m = jnp.maximum(m, c)
        return jnp.max(m)

    cs, contrib = _sb_tiles(q_subs, k_ref, v_ref, [qb * subs + r for r in range(subs)],
                            [jnp.zeros((blk, 1), F32)] * subs, tri, [ci < ri] * subs)
    for r in range(subs):
        o_ref[r * blk:(r + 1) * blk, :] = contrib[r]

    last = qb * subs + subs - 1

    def cond(carry):
        return jnp.logical_and(carry[0] <= last, carry[1])

    def body(carry):
        n = carry[0]
        kbs = [qb * subs + r - n for r in range(subs)]
        new, contrib = _sb_tiles(q_subs, k_ref, v_ref, [jnp.maximum(kb, 0) for kb in kbs],
                                 list(carry[2:]), tri, [kb >= 0 for kb in kbs])
        for r in range(subs):
            o_ref[r * blk:(r + 1) * blk, :] += contrib[r]
        return (n + 1, all_max(new) > EXP_ZERO_BELOW, *new)

    lax.while_loop(cond, body, (jnp.int32(1), all_max(cs) > EXP_ZERO_BELOW, *cs))


def _stick_breaking(p):
    t = p.shape[0]
    blk = SB_BLOCK * SB_SUBS
    return pl.pallas_call(
        _sb_kernel,
        grid=(N_MIX_HEADS, t // blk),
        in_specs=[pl.BlockSpec((blk, HEAD_DIM), lambda h, i: (i, h)),
                  pl.BlockSpec((t, HEAD_DIM), lambda h, i: (0, N_MIX_HEADS + h)),
                  pl.BlockSpec((t, HEAD_DIM), lambda h, i: (0, 2 * N_MIX_HEADS + h))],
        out_specs=pl.BlockSpec((blk, HEAD_DIM), lambda h, i: (i, h)),
        out_shape=jax.ShapeDtypeStruct((t, MIX_WIDTH), F32),
        compiler_params=_cparams("parallel", "parallel"),
        name="stick_breaking",
    )(p, p, p)


MEM_ROWS = 512


def _mem_attn_kernel(q_ref, kv_ref, o_ref):
    for h in range(MEM_HEADS):
        hd = slice(h * LANES, (h + 1) * LANES)
        km = kv_ref[:, hd].astype(BF16)
        vm = kv_ref[:, MEM_WIDTH + h * LANES:MEM_WIDTH + (h + 1) * LANES].astype(BF16)
        s = _dot_nt(q_ref[:, hd].astype(BF16), km) * (HEAD_DIM ** -0.5)
        s = s - jnp.max(s, axis=-1, keepdims=True)
        e = jnp.exp(s)
        prob = e / jnp.sum(e, axis=-1, keepdims=True)
        o_ref[:, hd] = _dot(prob.astype(BF16), vm)


def _mem_attn(p, kv, qm_col):
    t = p.shape[0]
    tb = MEM_ROWS
    return pl.pallas_call(
        _mem_attn_kernel,
        grid=(t // tb,),
        in_specs=[pl.BlockSpec((tb, MEM_WIDTH), lambda i: (i, qm_col // MEM_WIDTH)),
                  pl.BlockSpec((N_MEM, 2 * MEM_WIDTH), lambda i: (0, 0))],
        out_specs=pl.BlockSpec((tb, MEM_WIDTH), lambda i: (i, 0)),
        out_shape=jax.ShapeDtypeStruct((t, MEM_WIDTH), F32),
        compiler_params=_cparams("parallel"),
        name="mem_attn",
    )(p, kv)


OUT_ROWS = 256


def _out_ln_kernel(mix_ref, mem_ref, x_ref, w_ref, g_ref, b_ref, o_ref, ot_ref):
    y = _dot(mix_ref[...].astype(BF16), w_ref[0:MIX_WIDTH, :])
    y = y + _dot(mem_ref[...].astype(BF16), w_ref[MIX_WIDTH:, :])
    o = _layer_norm(DN_ALPHA * x_ref[...] + y, g_ref[...], b_ref[...])
    o_ref[...] = o
    ot_ref[...] = o.T.astype(BF16)


def _out_ln(mix, mem_o, x, w_out, g, b):
    t, d = x.shape
    tb = OUT_ROWS
    return pl.pallas_call(
        _out_ln_kernel,
        grid=(t // tb,),
        in_specs=[pl.BlockSpec((tb, MIX_WIDTH), lambda i: (i, 0)),
                  pl.BlockSpec((tb, MEM_WIDTH), lambda i: (i, 0)),
                  pl.BlockSpec((tb, d), lambda i: (i, 0)),
                  pl.BlockSpec((MIX_WIDTH + MEM_WIDTH, d), lambda i: (0, 0)),
                  pl.BlockSpec((1, d), lambda i: (0, 0)),
                  pl.BlockSpec((1, d), lambda i: (0, 0))],
        out_specs=[pl.BlockSpec((tb, d), lambda i: (i, 0)),
                   pl.BlockSpec((d, tb), lambda i: (0, i))],
        out_shape=[jax.ShapeDtypeStruct((t, d), F32),
                   jax.ShapeDtypeStruct((d, t), BF16)],
        compiler_params=_cparams("parallel"),
        name="out_proj_ln",
    )(mix, mem_o, x, w_out, g, b)


ROUTE_COLS = 256
PEER_HALF_KEEP = PEER_TOPK
assert PEER_HALF_KEEP <= PEER_HALF_TOPK
PEER_NCAND = PEER_HALF_KEEP * PEER_HALF_KEEP
PEER_SCORE_ROWS = PEER_HEADS * 2 * PEER_NKEYS
AUX_ROWS = 16
AUX_THR = 0
AUX_OFF = PEER_HEADS
AUX_TIE = 2 * PEER_HEADS


def _top_rows(vals, row_idx, count, limit, emit):
    for r in range(count):
        m = jnp.max(vals, axis=0, keepdims=True)
        am = jnp.min(jnp.where(vals == m, row_idx, float(limit)), axis=0, keepdims=True)
        sel = row_idx == am
        emit(r, m, am, sel)
        vals = jnp.where(sel, -jnp.inf, vals)
    return jnp.max(vals, axis=0, keepdims=True)


def _route_kernel(xt_ref, wq_ref, keys_ref, sc_ref, aux_ref, ids_ref, gates_ref,
                  sv_ref, ix_ref, cand_ref, eid_ref):
    tb = ROUTE_COLS
    qry = _dot(wq_ref[...], xt_ref[...])
    row_k = lax.broadcasted_iota(jnp.int32, (PEER_NKEYS, tb), 0).astype(F32)
    row_c = lax.broadcasted_iota(jnp.int32, (PEER_NCAND, tb), 0).astype(F32)
    tie = jnp.zeros((1, tb), F32)
    for h in range(PEER_HEADS):
        left = []
        for half in range(2):
            r0 = (h * 2 + half) * PEER_NKEYS
            s = _dot(keys_ref[half], qry[r0:r0 + PEER_NKEYS, :].astype(BF16))
            sc_ref[r0:r0 + PEER_NKEYS, :] = s

            def emit_half(r, m, am, sel, half=half):
                sv_ref[half, r:r + 1, :] = m
                ix_ref[half, r:r + 1, :] = am

            left.append(_top_rows(s, row_k, PEER_HALF_KEEP, PEER_NKEYS, emit_half))
        sv1, sv2 = sv_ref[0], sv_ref[1]
        ix1, ix2 = ix_ref[0], ix_ref[1]
        for a in range(PEER_HALF_KEEP):
            rows = slice(a * PEER_HALF_KEEP, (a + 1) * PEER_HALF_KEEP)
            cand_ref[rows, :] = sv1[a:a + 1, :] + sv2
            eid_ref[rows, :] = ix1[a:a + 1, :] * float(PEER_NKEYS) + ix2
        eid = eid_ref[...]
        cvs, eids = [], []

        def emit_cand(r, m, am, sel):
            cvs.append(m)
            eids.append(jnp.max(jnp.where(sel, eid, -1.0), axis=0, keepdims=True))

        runner_up = _top_rows(cand_ref[...], row_c, PEER_TOPK, PEER_NCAND, emit_cand)
        thr = cvs[PEER_TOPK - 1]
        outside = jnp.maximum(left[0] + sv2[0:1, :], sv1[0:1, :] + left[1])
        tie = jnp.maximum(tie, jnp.where(jnp.maximum(outside, runner_up) >= thr, 1.0, 0.0))
        exps = [jnp.exp(cv - cvs[0]) for cv in cvs]
        denom = exps[0]
        for e in exps[1:]:
            denom = denom + e
        aux_ref[AUX_THR + h:AUX_THR + h + 1, :] = thr
        aux_ref[AUX_OFF + h:AUX_OFF + h + 1, :] = -(cvs[0] + jnp.log(denom))
        for r in range(PEER_TOPK):
            slot = h * PEER_TOPK + r
            ids_ref[slot:slot + 1, :] = eids[r].astype(jnp.int32)
            gates_ref[slot:slot + 1, :] = exps[r] / denom
    aux_ref[AUX_TIE:AUX_TIE + 1, :] = tie
    aux_ref[AUX_TIE + 1:, :] = jnp.zeros((AUX_ROWS - AUX_TIE - 1, tb), F32)


def _route(xt, wq_t, keys):
    d, t = xt.shape
    tb = ROUTE_COLS
    col = lambda rows: pl.BlockSpec((rows, tb), lambda i: (0, i))
    return pl.pallas_call(
        _route_kernel,
        grid=(t // tb,),
        in_specs=[col(d),
                  pl.BlockSpec((PEER_SCORE_ROWS, d), lambda i: (0, 0)),
                  pl.BlockSpec((2, PEER_NKEYS, PEER_HALF), lambda i: (0, 0, 0))],
        out_specs=[col(PEER_SCORE_ROWS), col(AUX_ROWS), col(PEER_SLOTS), col(PEER_SLOTS)],
        out_shape=[jax.ShapeDtypeStruct((PEER_SCORE_ROWS, t), F32),
                   jax.ShapeDtypeStruct((AUX_ROWS, t), F32),
                   jax.ShapeDtypeStruct((PEER_SLOTS, t), jnp.int32),
                   jax.ShapeDtypeStruct((PEER_SLOTS, t), F32)],
        scratch_shapes=[pltpu.VMEM((2, PEER_HALF_KEEP, tb), F32),
                        pltpu.VMEM((2, PEER_HALF_KEEP, tb), F32),
                        pltpu.VMEM((PEER_NCAND, tb), F32),
                        pltpu.VMEM((PEER_NCAND, tb), F32)],
        compiler_params=_cparams("parallel"),
        name="peer_route",
    )(xt, wq_t, keys)


PEER_COLS = 512
PEER_ECHUNK = 1024

def _peer_kernel(xt_ref, x_ref, sc_ref, aux_ref, ids_ref, gates_ref, u_ref, vt_ref, g_ref, b_ref,
                 o_ref, acc_ref):
    j = pl.program_id(1)
    tb, ec = PEER_COLS, PEER_ECHUNK
    groups = ec // PEER_NKEYS

    @pl.when(j == 0)
    def _():
        acc_ref[...] = jnp.zeros_like(acc_ref)

    def threshold_gates(gi):
        w = jnp.zeros((PEER_NKEYS, tb), F32)
        for h in range(PEER_HEADS):
            base = h * 2 * PEER_NKEYS
            s1 = sc_ref[pl.ds(base + j * groups + gi, 1), :]
            s2 = sc_ref[base + PEER_NKEYS:base + 2 * PEER_NKEYS, :]
            dsum = s1 + s2
            gate = jnp.exp(dsum + aux_ref[AUX_OFF + h:AUX_OFF + h + 1, :])
            w = w + jnp.where(dsum >= aux_ref[AUX_THR + h:AUX_THR + h + 1, :], gate, 0.0)
        return w

    def scattered_gates(gi):
        row_e = lax.broadcasted_iota(jnp.int32, (PEER_NKEYS, tb), 0) + (j * ec + gi * PEER_NKEYS)
        w = jnp.zeros((PEER_NKEYS, tb), F32)
        for s in range(PEER_SLOTS):
            w = w + jnp.where(ids_ref[s:s + 1, :] == row_e, gates_ref[s:s + 1, :], 0.0)
        return w

    def step(gates_of):
        xt = xt_ref[...]
        acts = []
        for gi in range(groups):
            rows = slice(gi * PEER_NKEYS, (gi + 1) * PEER_NKEYS)
            hid = _dot(u_ref[rows, :], xt)
            gelu = 0.5 * hid * (1.0 + lax.erf(hid * (2.0 ** -0.5)))
            acts.append((gelu * gates_of(gi)).astype(BF16))
        acc_ref[...] += _dot(vt_ref[...], jnp.concatenate(acts, axis=0))

    any_tie = jnp.max(aux_ref[AUX_TIE:AUX_TIE + 1, :]) > 0.0
    pl.when(jnp.logical_not(any_tie))(lambda: step(threshold_gates))
    pl.when(any_tie)(lambda: step(scattered_gates))

    @pl.when(j == pl.num_programs(1) - 1)
    def _():
        o_ref[...] = _layer_norm(DN_ALPHA * x_ref[...] + acc_ref[...].T, g_ref[...], b_ref[...])


def _peer(xt, x, scores, aux, ids, gates, u, vt, g, b):
    t, d = x.shape
    tb, ec = PEER_COLS, PEER_ECHUNK
    ne = u.shape[0]
    col = lambda rows: pl.BlockSpec((rows, tb), lambda i, j: (0, i))
    return pl.pallas_call(
        _peer_kernel,
        grid=(t // tb, ne // ec),
        in_specs=[col(d),
                  pl.BlockSpec((tb, d), lambda i, j: (i, 0)),
                  col(PEER_SCORE_ROWS), col(AUX_ROWS), col(PEER_SLOTS), col(PEER_SLOTS),
                  pl.BlockSpec((ec, d), lambda i, j: (j, 0)),
                  pl.BlockSpec((d, ec), lambda i, j: (0, j)),
                  pl.BlockSpec((1, d), lambda i, j: (0, 0)),
                  pl.BlockSpec((1, d), lambda i, j: (0, 0))],
        out_specs=pl.BlockSpec((tb, d), lambda i, j: (i, 0)),
        out_shape=jax.ShapeDtypeStruct((t, d), F32),
        scratch_shapes=[pltpu.VMEM((d, tb), F32)],
        compiler_params=_cparams("parallel", "arbitrary"),
        name="peer_dense",
    )(xt, x, scores, aux, ids, gates, u, vt, g, b)


def _pad_lanes(vec):
    return jnp.zeros((1, LANES), F32).at[0, :vec.shape[0]].set(vec.astype(F32))


def _gdn_weight(w):
    qw, vw, nh = GDN_QK_WIDTH, MIX_WIDTH, N_MIX_HEADS
    d = w.shape[0]
    o_b = 2 * qw + 2 * vw
    main = w[:, :o_b]
    b_cols = w[:, o_b:o_b + nh]
    a_cols = w[:, o_b + nh:o_b + 2 * nh]
    qm = w[:, o_b + 2 * nh:]
    zpad = jnp.zeros((d, LANES - nh), w.dtype)
    return jnp.concatenate([main, qm, b_cols, zpad, a_cols, zpad], axis=1).astype(BF16)


def _forward(x, mem, gdn_w_in, gdn_conv, gdn_a_log, gdn_dt_bias, gdn_onorm, sb_w_in, mem_w_kv, w_out,
             ln_mix_g, ln_mix_b, peer_w_q, peer_keys, peer_u, peer_v, ln_ffn_g, ln_ffn_b):
    t = x.shape[1]
    d = x.shape[2]
    xt = x.reshape(t, d)
    memt = mem.reshape(N_MEM, d)
    row = lambda a: a.reshape(1, -1).astype(F32)
    for i in range(DEPTH):
        li = i // 2
        if i % 2 == 0:
            p = _matmul(xt, _gdn_weight(gdn_w_in[li]), 512, 768)
            qn, kn, vc, beta, gcum = _gdn_prep(p, gdn_conv[li].astype(F32),
                                               _pad_lanes(gdn_a_log[li]), _pad_lanes(gdn_dt_bias[li]))
            u, w, qg, kd, aqk, egl = _gdn_chunk(qn, kn, vc, beta, gcum)
            mix = _gdn_scan(u, w, qg, kd, aqk, egl, p, row(gdn_onorm[li]))
            qm_col = GDN_COL_QM
        else:
            p = _matmul(xt, sb_w_in[li].astype(BF16), 512, 512)
            mix = _stick_breaking(p)
            qm_col = SB_COL_QM
        kv = _matmul(memt, mem_w_kv[i].astype(BF16), N_MEM, 512)
        mem_o = _mem_attn(p, kv, qm_col)
        x1, x1_t = _out_ln(mix, mem_o, xt, w_out[i].astype(BF16), row(ln_mix_g[i]), row(ln_mix_b[i]))
        scores, aux, ids, gates = _route(x1_t, peer_w_q[i].T.astype(BF16), peer_keys[i].astype(BF16))
        xt = _peer(x1_t, x1, scores, aux, ids, gates, peer_u[i].astype(BF16), peer_v[i].T.astype(BF16),
                   row(ln_ffn_g[i]), row(ln_ffn_b[i]))
    return xt.reshape(x.shape)


def kernel(x, mem, gdn_w_in, gdn_conv, gdn_a_log, gdn_dt_bias, gdn_onorm, sb_w_in, mem_w_kv, w_out,
           ln_mix_g, ln_mix_b, peer_w_q, peer_keys, peer_u, peer_v, ln_ffn_g, ln_ffn_b):
    return _forward(x, mem, gdn_w_in, gdn_conv, gdn_a_log, gdn_dt_bias, gdn_onorm, sb_w_in, mem_w_kv,
                    w_out, ln_mix_g, ln_mix_b, peer_w_q, peer_keys, peer_u, peer_v, ln_ffn_g, ln_ffn_b)
```

```python
import functools
import math

import jax
import jax.numpy as jnp
from jax import lax
from jax.experimental import pallas as pl
from jax.experimental.pallas import tpu as pltpu

F32 = jnp.float32
BF16 = jnp.bfloat16
HI = lax.Precision.HIGHEST

D_MODEL = 2048
DEPTH = 4
HEAD_DIM = 128
N_MIX_HEADS = 8
GDN_QK_HEADS = 4
MIX_WIDTH = N_MIX_HEADS * HEAD_DIM
GDN_QK_WIDTH = GDN_QK_HEADS * HEAD_DIM
N_MEM = 256
MEM_HEADS = 4
MEM_WIDTH = MEM_HEADS * HEAD_DIM
CONV_WIDTH = 4
GDN_CHUNK = 64
SB_BLOCK = 128
PEER_HEADS = 4
PEER_NKEYS = 128
PEER_EXPERTS = PEER_NKEYS * PEER_NKEYS
PEER_HALF = 128
PEER_HALF_TOPK = 16
PEER_TOPK = 8
PEER_SLOTS = PEER_HEADS * PEER_TOPK
DN_ALPHA = (2.0 * DEPTH) ** 0.25
LN_EPS = 1e-5
RMS_EPS = 1e-6
LANES = 128
SUBLANES = 8
EXP_ZERO_BELOW = -104.0

GDN_P_WIDTH = 3840
GDN_COL_Z = 2048
GDN_COL_QM = 3072
GDN_COL_B = 3584
GDN_COL_A = 3712
SB_P_WIDTH = 3 * MIX_WIDTH + MEM_WIDTH
SB_COL_QM = 3 * MIX_WIDTH

VMEM_LIMIT = 56 * 1024 * 1024


def _cparams(*sem):
    return pltpu.CompilerParams(dimension_semantics=sem, vmem_limit_bytes=VMEM_LIMIT)


def _dot(a, b):
    return jnp.dot(a, b, preferred_element_type=F32)


def _dot_nt(a, b):
    return lax.dot_general(a, b, (((1,), (1,)), ((), ())), preferred_element_type=F32)


def _dot_tn(a, b):
    return lax.dot_general(a, b, (((0,), (0,)), ((), ())), preferred_element_type=F32)


def _dot_hi(a, b):
    return jnp.dot(a, b, precision=HI, preferred_element_type=F32)


def _split2(a):
    hi = a.astype(BF16)
    lo = (a - hi.astype(F32)).astype(BF16)
    return hi, lo


def _dot3(a2, b2):
    return _dot(a2[0], b2[0]) + (_dot(a2[0], b2[1]) + _dot(a2[1], b2[0]))


def _softplus(x):
    return jnp.maximum(x, 0.0) + jnp.log1p(jnp.exp(-jnp.abs(x)))


def _silu(x):
    return x * jax.nn.sigmoid(x)


def _layer_norm(r, g, b):
    mu = jnp.mean(r, axis=-1, keepdims=True)
    d = r - mu
    var = jnp.mean(d * d, axis=-1, keepdims=True)
    return d * lax.rsqrt(var + LN_EPS) * g + b


def _mm_kernel(x_ref, w_ref, o_ref):
    o_ref[...] = _dot(x_ref[...].astype(BF16), w_ref[...])


def _matmul(x, w, tm, tn):
    m, k = x.shape
    n = w.shape[1]
    return pl.pallas_call(
        _mm_kernel,
        grid=(m // tm, n // tn),
        in_specs=[pl.BlockSpec((tm, k), lambda i, j: (i, 0)),
                  pl.BlockSpec((k, tn), lambda i, j: (0, j))],
        out_specs=pl.BlockSpec((tm, tn), lambda i, j: (i, j)),
        out_shape=jax.ShapeDtypeStruct((m, n), F32),
        compiler_params=_cparams("parallel", "parallel"),
        name="proj_matmul",
    )(x, w)


GDN_PREP_ROWS = 256


def _gdn_prep_kernel(pc_ref, ph_ref, pb_ref, pa_ref, cw_ref, alog_ref, dtb_ref,
                     q_ref, k_ref, v_ref, beta_ref, gcum_ref, xbuf):
    tb = GDN_PREP_ROWS
    first = pl.program_id(0) == 0
    halo = ph_ref[...]
    xbuf[0:SUBLANES, :] = jnp.where(first, 0.0, halo)
    xbuf[SUBLANES:, :] = pc_ref[...]
    n_groups = (2 * GDN_QK_WIDTH + MIX_WIDTH) // LANES
    for gi in range(n_groups):
        cols = slice(gi * LANES, (gi + 1) * LANES)
        y = jnp.zeros((tb, LANES), F32)
        for j in range(CONV_WIDTH):
            r0 = SUBLANES - (CONV_WIDTH - 1) + j
            tap = xbuf[r0:r0 + tb, cols]
            y = y + tap * cw_ref[j:j + 1, cols]
        y = _silu(y)
        if gi < 2 * GDN_QK_HEADS:
            y = y * lax.rsqrt(jnp.sum(y * y, axis=-1, keepdims=True) + RMS_EPS)
        if gi < GDN_QK_HEADS:
            q_ref[:, cols] = y * (HEAD_DIM ** -0.5)
        elif gi < 2 * GDN_QK_HEADS:
            k_ref[:, (gi - GDN_QK_HEADS) * LANES:(gi - GDN_QK_HEADS + 1) * LANES] = y
        else:
            v_ref[:, (gi - 2 * GDN_QK_HEADS) * LANES:(gi - 2 * GDN_QK_HEADS + 1) * LANES] = y
    beta_ref[...] = jax.nn.sigmoid(pb_ref[...])
    g = -jnp.exp(alog_ref[...]) * _softplus(pa_ref[...] + dtb_ref[...])
    r = lax.broadcasted_iota(jnp.int32, (tb, tb), 0)
    c = lax.broadcasted_iota(jnp.int32, (tb, tb), 1)
    shift = GDN_CHUNK.bit_length() - 1
    same = lax.shift_right_logical(r, shift) == lax.shift_right_logical(c, shift)
    tri = jnp.where(same & (c <= r), 1.0, 0.0).astype(F32)
    gcum_ref[...] = _dot_hi(tri, g)


def _gdn_prep(p, conv_w, alog_pad, dtb_pad):
    t = p.shape[0]
    tb = GDN_PREP_ROWS
    cw = 2 * GDN_QK_WIDTH + MIX_WIDTH
    per8 = tb // SUBLANES
    return pl.pallas_call(
        _gdn_prep_kernel,
        grid=(t // tb,),
        in_specs=[
            pl.BlockSpec((tb, cw), lambda i: (i, 0)),
            pl.BlockSpec((SUBLANES, cw), lambda i: (jnp.maximum(i * per8 - 1, 0), 0)),
            pl.BlockSpec((tb, LANES), lambda i: (i, GDN_COL_B // LANES)),
            pl.BlockSpec((tb, LANES), lambda i: (i, GDN_COL_A // LANES)),
            pl.BlockSpec((CONV_WIDTH, cw), lambda i: (0, 0)),
            pl.BlockSpec((1, LANES), lambda i: (0, 0)),
            pl.BlockSpec((1, LANES), lambda i: (0, 0)),
        ],
        out_specs=[
            pl.BlockSpec((tb, GDN_QK_WIDTH), lambda i: (i, 0)),
            pl.BlockSpec((tb, GDN_QK_WIDTH), lambda i: (i, 0)),
            pl.BlockSpec((tb, MIX_WIDTH), lambda i: (i, 0)),
            pl.BlockSpec((tb, LANES), lambda i: (i, 0)),
            pl.BlockSpec((tb, LANES), lambda i: (i, 0)),
        ],
        out_shape=[
            jax.ShapeDtypeStruct((t, GDN_QK_WIDTH), F32),
            jax.ShapeDtypeStruct((t, GDN_QK_WIDTH), F32),
            jax.ShapeDtypeStruct((t, MIX_WIDTH), F32),
            jax.ShapeDtypeStruct((t, LANES), F32),
            jax.ShapeDtypeStruct((t, LANES), F32),
        ],
        scratch_shapes=[pltpu.VMEM((tb + SUBLANES, cw), F32)],
        compiler_params=_cparams("parallel"),
        name="gdn_prep",
    )(p, p, p, p, conv_w, alog_pad, dtb_pad)


GDN_PAIR = 2


def _gdn_chunk_kernel(q_ref, k_ref, v_ref, beta_ref, gcum_ref,
                      u_ref, w_ref, qg_ref, kd_ref, aqk_ref, egl_ref):
    c = GDN_CHUNK * GDN_PAIR
    shift = GDN_CHUNK.bit_length() - 1
    beta = beta_ref[...]
    gcum = gcum_ref[...]
    gcum_t = gcum.T
    for ck in range(GDN_PAIR):
        last = (ck + 1) * GDN_CHUNK - 1
        glast_col = gcum_t[:, last:last + 1]
        egl_ref[ck * N_MIX_HEADS:(ck + 1) * N_MIX_HEADS, :] = jnp.exp(
            jnp.broadcast_to(glast_col, (LANES, LANES))[0:N_MIX_HEADS, :])
    ri = lax.broadcasted_iota(jnp.int32, (c, c), 0)
    ci = lax.broadcasted_iota(jnp.int32, (c, c), 1)
    same = lax.shift_right_logical(ri, shift) == lax.shift_right_logical(ci, shift)
    causal = same & (ci <= ri)
    strict = same & (ci < ri)
    eye = jnp.where(ci == ri, 1.0, 0.0).astype(F32)
    row_chunk = lax.shift_right_logical(lax.broadcasted_iota(jnp.int32, (c, 1), 0), shift)
    rep = N_MIX_HEADS // GDN_QK_HEADS
    heads = range(N_MIX_HEADS)
    head_cols = [slice(h * LANES, (h + 1) * LANES) for h in heads]
    q_h = [q_ref[:, hq * LANES:(hq + 1) * LANES] for hq in range(GDN_QK_HEADS)]
    k_h = [k_ref[:, hq * LANES:(hq + 1) * LANES] for hq in range(GDN_QK_HEADS)]
    k_b = [k.astype(BF16) for k in k_h]
    qk = [_dot_nt(q.astype(BF16), kb16) for q, kb16 in zip(q_h, k_b)]
    g_col = [gcum[:, h:h + 1] for h in heads]
    b_col = [beta[:, h:h + 1] for h in heads]
    decay = [jnp.exp(jnp.where(causal, g_col[h] - gcum_t[h:h + 1, :], -jnp.inf)) for h in heads]
    eg = [jnp.exp(g) for g in g_col]
    kbeta = [k_h[h // rep] * b_col[h] for h in heads]
    kk = [_dot_nt(kbeta[h].astype(BF16), k_b[h // rep]) for h in heads]
    nm = [jnp.where(strict, -(kk[h] * decay[h]), 0.0) for h in heads]
    inv = [eye + n for n in nm]
    pw2 = [_split2(n) for n in nm]
    for _ in range(5):
        pw2 = [_split2(_dot3(p2, p2)) for p2 in pw2]
        inv = [inv[h] + _dot3(pw2[h], _split2(inv[h])) for h in heads]
    for h in heads:
        hd = head_cols[h]
        rhs = jnp.concatenate([v_ref[:, hd] * b_col[h], kbeta[h] * eg[h]], axis=1)
        sol = _dot3(_split2(inv[h]), _split2(rhs))
        u_ref[:, hd] = sol[:, :LANES]
        w_ref[:, hd] = sol[:, LANES:].astype(BF16)
        qg_ref[:, hd] = (q_h[h // rep] * eg[h]).astype(BF16)
        g_last = gcum[GDN_CHUNK - 1:GDN_CHUNK, h:h + 1]
        for ck in range(1, GDN_PAIR):
            last = (ck + 1) * GDN_CHUNK - 1
            g_last = jnp.where(row_chunk == ck, gcum[last:last + 1, h:h + 1], g_last)
        kd_ref[:, hd] = (k_h[h // rep] * jnp.exp(g_last - g_col[h])).astype(BF16)
        aqk = (qk[h // rep] * decay[h]).astype(BF16)
        for ck in range(GDN_PAIR):
            blk = slice(ck * GDN_CHUNK, (ck + 1) * GDN_CHUNK)
            aqk_ref[ck, h] = aqk[blk, blk]


def _gdn_chunk(qn, kn, vc, beta, gcum):
    t = qn.shape[0]
    c = GDN_CHUNK
    n = t // c
    rows = c * GDN_PAIR
    row = lambda w: pl.BlockSpec((rows, w), lambda i: (i, 0))
    return pl.pallas_call(
        _gdn_chunk_kernel,
        grid=(n // GDN_PAIR,),
        in_specs=[row(GDN_QK_WIDTH), row(GDN_QK_WIDTH), row(MIX_WIDTH), row(LANES), row(LANES)],
        out_specs=[row(MIX_WIDTH), row(MIX_WIDTH), row(MIX_WIDTH), row(MIX_WIDTH),
                   pl.BlockSpec((GDN_PAIR, N_MIX_HEADS, c, c), lambda i: (i, 0, 0, 0)),
                   pl.BlockSpec((GDN_PAIR * N_MIX_HEADS, LANES), lambda i: (i, 0))],
        out_shape=[
            jax.ShapeDtypeStruct((t, MIX_WIDTH), F32),
            jax.ShapeDtypeStruct((t, MIX_WIDTH), BF16),
            jax.ShapeDtypeStruct((t, MIX_WIDTH), BF16),
            jax.ShapeDtypeStruct((t, MIX_WIDTH), BF16),
            jax.ShapeDtypeStruct((n, N_MIX_HEADS, c, c), BF16),
            jax.ShapeDtypeStruct((n * N_MIX_HEADS, LANES), F32),
        ],
        compiler_params=_cparams("parallel"),
        name="gdn_chunk",
    )(qn, kn, vc, beta, gcum)


GDN_SCAN_CHUNKS = 4


def _gdn_scan_kernel(u_ref, w_ref, qg_ref, kd_ref, aqk_ref, egl_ref, z_ref, on_ref,
                     o_ref, s_ref):
    @pl.when(pl.program_id(0) == 0)
    def _():
        s_ref[...] = jnp.zeros_like(s_ref)

    on = on_ref[...]
    c = GDN_CHUNK
    heads = range(N_MIX_HEADS)
    cols = [slice(h * LANES, (h + 1) * LANES) for h in heads]
    state = [s_ref[h] for h in heads]
    for ck in range(GDN_SCAN_CHUNKS):
        rows = slice(ck * c, (ck + 1) * c)
        s_b = [s.astype(BF16) for s in state]
        ws = [_dot(w_ref[rows, cols[h]], s_b[h]) for h in heads]
        qs = [_dot(qg_ref[rows, cols[h]], s_b[h]) for h in heads]
        v_b = [(u_ref[rows, cols[h]] - ws[h]).astype(BF16) for h in heads]
        av = [_dot(aqk_ref[ck, h], v_b[h]) for h in heads]
        kv = [_dot_tn(kd_ref[rows, cols[h]], v_b[h]) for h in heads]
        egl = egl_ref[ck * N_MIX_HEADS:(ck + 1) * N_MIX_HEADS, :]
        state = [state[h] * egl[h:h + 1, :] + kv[h] for h in heads]
        for h in heads:
            o = qs[h] + av[h]
            o = o * lax.rsqrt(jnp.mean(o * o, axis=-1, keepdims=True) + RMS_EPS) * on
            o_ref[rows, cols[h]] = o * _silu(z_ref[rows, cols[h]])
    for h in heads:
        s_ref[h] = state[h]


def _gdn_scan(u, w, qg, kd, aqk, egl, p, onorm):
    t = u.shape[0]
    nck = GDN_SCAN_CHUNKS
    c = GDN_CHUNK * nck
    n = t // c
    row = lambda wd: pl.BlockSpec((c, wd), lambda i: (i, 0))
    return pl.pallas_call(
        _gdn_scan_kernel,
        grid=(n,),
        in_specs=[row(MIX_WIDTH), row(MIX_WIDTH), row(MIX_WIDTH), row(MIX_WIDTH),
                  pl.BlockSpec((nck, N_MIX_HEADS, GDN_CHUNK, GDN_CHUNK), lambda i: (i, 0, 0, 0)),
                  pl.BlockSpec((nck * N_MIX_HEADS, LANES), lambda i: (i, 0)),
                  pl.BlockSpec((c, MIX_WIDTH), lambda i: (i, GDN_COL_Z // MIX_WIDTH)),
                  pl.BlockSpec((1, LANES), lambda i: (0, 0))],
        out_specs=row(MIX_WIDTH),
        out_shape=jax.ShapeDtypeStruct((t, MIX_WIDTH), F32),
        scratch_shapes=[pltpu.VMEM((N_MIX_HEADS, HEAD_DIM, HEAD_DIM), F32)],
        compiler_params=_cparams("arbitrary"),
        name="gdn_scan",
    )(u, w, qg, kd, aqk, egl, p, onorm)


SB_SUBS = 8


def _sb_tiles(q_subs, k_ref, v_ref, kbs, c_runs, tri, keeps):
    blk = SB_BLOCK
    subs = range(len(q_subs))
    starts = [pl.multiple_of(kb * blk, blk) for kb in kbs]
    zs = [_dot_nt(q_subs[r], k_ref[pl.ds(starts[r], blk), :].astype(BF16)) for r in subs]
    ts = [jnp.log1p(jnp.exp(-jnp.abs(z))) for z in zs]
    log_not = [jnp.where(keeps[r], -jnp.maximum(zs[r], 0.0) - ts[r], 0.0) for r in subs]
    log_beta = [jnp.minimum(zs[r], 0.0) - ts[r] for r in subs]
    hi = [x.astype(BF16) for x in log_not]
    rest = [log_not[r] - hi[r].astype(F32) for r in subs]
    mid = [x.astype(BF16) for x in rest]
    lo = [(rest[r] - mid[r].astype(F32)).astype(BF16) for r in subs]
    suffix = [_dot(hi[r], tri) + (_dot(mid[r], tri) + _dot(lo[r], tri)) for r in subs]
    att = [jnp.where(keeps[r], jnp.exp(log_beta[r] + suffix[r] + c_runs[r]), 0.0).astype(BF16) for r in subs]
    contrib = [_dot(att[r], v_ref[pl.ds(starts[r], blk), :].astype(BF16)) for r in subs]
    c_new = [c_runs[r] + jnp.sum(log_not[r], axis=-1, keepdims=True) for r in subs]
    return c_new, contrib


def _sb_kernel(q_ref, k_ref, v_ref, o_ref):
    blk, subs = SB_BLOCK, SB_SUBS
    qb = pl.program_id(1)
    ri = lax.broadcasted_iota(jnp.int32, (blk, blk), 0)
    ci = lax.broadcasted_iota(jnp.int32, (blk, blk), 1)
    tri = jnp.where(ri > ci, 1.0, 0.0).astype(BF16)
    q_subs = [(q_ref[r * blk:(r + 1) * blk, :] * (HEAD_DIM ** -0.5)).astype(BF16) for r in range(subs)]

    def all_max(cs):
        m = cs[0]
        for c in cs[1:]:
            m = jnp.maximum(m, c)
        return jnp.max(m)

    cs, contrib = _sb_tiles(q_subs, k_ref, v_ref, [qb * subs + r for r in range(subs)],
                            [jnp.zeros((blk, 1), F32)] * subs, tri, [ci < ri] * subs)
    for r in range(subs):
        o_ref[r * blk:(r + 1) * blk, :] = contrib[r]

    last = qb * subs + subs - 1

    def cond(carry):
        return jnp.logical_and(carry[0] <= last, carry[1])

    def body(carry):
        n = carry[0]
        kbs = [qb * subs + r - n for r in range(subs)]
        new, contrib = _sb_tiles(q_subs, k_ref, v_ref, [jnp.maximum(kb, 0) for kb in kbs],
                                 list(carry[2:]), tri, [kb >= 0 for kb in kbs])
        for r in range(subs):
            o_ref[r * blk:(r + 1) * blk, :] += contrib[r]
        return (n + 1, all_max(new) > EXP_ZERO_BELOW, *new)

    lax.while_loop(cond, body, (jnp.int32(1), all_max(cs) > EXP_ZERO_BELOW, *cs))


def _stick_breaking(p):
    t = p.shape[0]
    blk = SB_BLOCK * SB_SUBS
    return pl.pallas_call(
        _sb_kernel,
        grid=(N_MIX_HEADS, t // blk),
        in_specs=[pl.BlockSpec((blk, HEAD_DIM), lambda h, i: (i, h)),
                  pl.BlockSpec((t, HEAD_DIM), lambda h, i: (0, N_MIX_HEADS + h)),
                  pl.BlockSpec((t, HEAD_DIM), lambda h, i: (0, 2 * N_MIX_HEADS + h))],
        out_specs=pl.BlockSpec((blk, HEAD_DIM), lambda h, i: (i, h)),
        out_shape=jax.ShapeDtypeStruct((t, MIX_WIDTH), F32),
        compiler_params=_cparams("parallel", "parallel"),
        name="stick_breaking",
    )(p, p, p)


MEM_ROWS = 512


def _mem_attn_kernel(q_ref, kv_ref, o_ref):
    for h in range(MEM_HEADS):
        hd = slice(h * LANES, (h + 1) * LANES)
        km = kv_ref[:, hd].astype(BF16)
        vm = kv_ref[:, MEM_WIDTH + h * LANES:MEM_WIDTH + (h + 1) * LANES].astype(BF16)
        s = _dot_nt(q_ref[:, hd].astype(BF16), km) * (HEAD_DIM ** -0.5)
        s = s - jnp.max(s, axis=-1, keepdims=True)
        e = jnp.exp(s)
        prob = e / jnp.sum(e, axis=-1, keepdims=True)
        o_ref[:, hd] = _dot(prob.astype(BF16), vm)


def _mem_attn(p, kv, qm_col):
    t = p.shape[0]
    tb = MEM_ROWS
    return pl.pallas_call(
        _mem_attn_kernel,
        grid=(t // tb,),
        in_specs=[pl.BlockSpec((tb, MEM_WIDTH), lambda i: (i, qm_col // MEM_WIDTH)),
                  pl.BlockSpec((N_MEM, 2 * MEM_WIDTH), lambda i: (0, 0))],
        out_specs=pl.BlockSpec((tb, MEM_WIDTH), lambda i: (i, 0)),
        out_shape=jax.ShapeDtypeStruct((t, MEM_WIDTH), F32),
        compiler_params=_cparams("parallel"),
        name="mem_attn",
    )(p, kv)


OUT_ROWS = 512


def _out_ln_kernel(mix_ref, mem_ref, x_ref, w_ref, g_ref, b_ref, o_ref, ot_ref):
    y = _dot(mix_ref[...].astype(BF16), w_ref[0:MIX_WIDTH, :])
    y = y + _dot(mem_ref[...].astype(BF16), w_ref[MIX_WIDTH:, :])
    o = _layer_norm(DN_ALPHA * x_ref[...] + y, g_ref[...], b_ref[...])
    o_ref[...] = o
    ot_ref[...] = o.T.astype(BF16)


def _out_ln(mix, mem_o, x, w_out, g, b):
    t, d = x.shape
    tb = OUT_ROWS
    return pl.pallas_call(
        _out_ln_kernel,
        grid=(t // tb,),
        in_specs=[pl.BlockSpec((tb, MIX_WIDTH), lambda i: (i, 0)),
                  pl.BlockSpec((tb, MEM_WIDTH), lambda i: (i, 0)),
                  pl.BlockSpec((tb, d), lambda i: (i, 0)),
                  pl.BlockSpec((MIX_WIDTH + MEM_WIDTH, d), lambda i: (0, 0)),
                  pl.BlockSpec((1, d), lambda i: (0, 0)),
                  pl.BlockSpec((1, d), lambda i: (0, 0))],
        out_specs=[pl.BlockSpec((tb, d), lambda i: (i, 0)),
                   pl.BlockSpec((d, tb), lambda i: (0, i))],
        out_shape=[jax.ShapeDtypeStruct((t, d), F32),
                   jax.ShapeDtypeStruct((d, t), BF16)],
        compiler_params=_cparams("parallel"),
        name="out_proj_ln",
    )(mix, mem_o, x, w_out, g, b)


ROUTE_COLS = 512
PEER_HALF_KEEP = PEER_TOPK
assert PEER_HALF_KEEP <= PEER_HALF_TOPK
PEER_NCAND = PEER_HALF_KEEP * PEER_HALF_KEEP
PEER_SCORE_ROWS = PEER_HEADS * 2 * PEER_NKEYS
AUX_ROWS = 16
AUX_THR = 0
AUX_OFF = PEER_HEADS
AUX_TIE = 2 * PEER_HEADS


def _top_rows(vals, row_idx, count, limit, emit):
    for r in range(count):
        m = jnp.max(vals, axis=0, keepdims=True)
        am = jnp.min(jnp.where(vals == m, row_idx, float(limit)), axis=0, keepdims=True)
        sel = row_idx == am
        emit(r, m, am, sel)
        vals = jnp.where(sel, -jnp.inf, vals)
    return jnp.max(vals, axis=0, keepdims=True)


def _route_kernel(xt_ref, wq_ref, keys_ref, sc_ref, aux_ref, ids_ref, gates_ref,
                  sv_ref, ix_ref, cand_ref, eid_ref):
    tb = ROUTE_COLS
    qry = _dot(wq_ref[...], xt_ref[...])
    row_k = lax.broadcasted_iota(jnp.int32, (PEER_NKEYS, tb), 0).astype(F32)
    row_c = lax.broadcasted_iota(jnp.int32, (PEER_NCAND, tb), 0).astype(F32)
    tie = jnp.zeros((1, tb), F32)
    for h in range(PEER_HEADS):
        left = []
        for half in range(2):
            r0 = (h * 2 + half) * PEER_NKEYS
            s = _dot(keys_ref[half], qry[r0:r0 + PEER_NKEYS, :].astype(BF16))
            sc_ref[r0:r0 + PEER_NKEYS, :] = s

            def emit_half(r, m, am, sel, half=half):
                sv_ref[half, r:r + 1, :] = m
                ix_ref[half, r:r + 1, :] = am

            left.append(_top_rows(s, row_k, PEER_HALF_KEEP, PEER_NKEYS, emit_half))
        sv1, sv2 = sv_ref[0], sv_ref[1]
        ix1, ix2 = ix_ref[0], ix_ref[1]
        for a in range(PEER_HALF_KEEP):
            rows = slice(a * PEER_HALF_KEEP, (a + 1) * PEER_HALF_KEEP)
            cand_ref[rows, :] = sv1[a:a + 1, :] + sv2
            eid_ref[rows, :] = ix1[a:a + 1, :] * float(PEER_NKEYS) + ix2
        eid = eid_ref[...]
        cvs, eids = [], []

        def emit_cand(r, m, am, sel):
            cvs.append(m)
            eids.append(jnp.max(jnp.where(sel, eid, -1.0), axis=0, keepdims=True))

        runner_up = _top_rows(cand_ref[...], row_c, PEER_TOPK, PEER_NCAND, emit_cand)
        thr = cvs[PEER_TOPK - 1]
        outside = jnp.maximum(left[0] + sv2[0:1, :], sv1[0:1, :] + left[1])
        tie = jnp.maximum(tie, jnp.where(jnp.maximum(outside, runner_up) >= thr, 1.0, 0.0))
        exps = [jnp.exp(cv - cvs[0]) for cv in cvs]
        denom = exps[0]
        for e in exps[1:]:
            denom = denom + e
        aux_ref[AUX_THR + h:AUX_THR + h + 1, :] = thr
        aux_ref[AUX_OFF + h:AUX_OFF + h + 1, :] = -(cvs[0] + jnp.log(denom))
        for r in range(PEER_TOPK):
            slot = h * PEER_TOPK + r
            ids_ref[slot:slot + 1, :] = eids[r].astype(jnp.int32)
            gates_ref[slot:slot + 1, :] = exps[r] / denom
    aux_ref[AUX_TIE:AUX_TIE + 1, :] = tie
    aux_ref[AUX_TIE + 1:, :] = jnp.zeros((AUX_ROWS - AUX_TIE - 1, tb), F32)


def _route(xt, wq_t, keys):
    d, t = xt.shape
    tb = ROUTE_COLS
    col = lambda rows: pl.BlockSpec((rows, tb), lambda i: (0, i))
    return pl.pallas_call(
        _route_kernel,
        grid=(t // tb,),
        in_specs=[col(d),
                  pl.BlockSpec((PEER_SCORE_ROWS, d), lambda i: (0, 0)),
                  pl.BlockSpec((2, PEER_NKEYS, PEER_HALF), lambda i: (0, 0, 0))],
        out_specs=[col(PEER_SCORE_ROWS), col(AUX_ROWS), col(PEER_SLOTS), col(PEER_SLOTS)],
        out_shape=[jax.ShapeDtypeStruct((PEER_SCORE_ROWS, t), F32),
                   jax.ShapeDtypeStruct((AUX_ROWS, t), F32),
                   jax.ShapeDtypeStruct((PEER_SLOTS, t), jnp.int32),
                   jax.ShapeDtypeStruct((PEER_SLOTS, t), F32)],
        scratch_shapes=[pltpu.VMEM((2, PEER_HALF_KEEP, tb), F32),
                        pltpu.VMEM((2, PEER_HALF_KEEP, tb), F32),
                        pltpu.VMEM((PEER_NCAND, tb), F32),
                        pltpu.VMEM((PEER_NCAND, tb), F32)],
        compiler_params=_cparams("parallel"),
        name="peer_route",
    )(xt, wq_t, keys)


PEER_COLS = 512
PEER_ECHUNK = 1024

def _peer_kernel(xt_ref, x_ref, sc_ref, aux_ref, ids_ref, gates_ref, u_ref, vt_ref, g_ref, b_ref,
                 o_ref, acc_ref):
    j = pl.program_id(1)
    tb, ec = PEER_COLS, PEER_ECHUNK
    groups = ec // PEER_NKEYS

    @pl.when(j == 0)
    def _():
        acc_ref[...] = jnp.zeros_like(acc_ref)

    def threshold_gates(gi):
        w = jnp.zeros((PEER_NKEYS, tb), F32)
        for h in range(PEER_HEADS):
            base = h * 2 * PEER_NKEYS
            s1 = sc_ref[pl.ds(base + j * groups + gi, 1), :]
            s2 = sc_ref[base + PEER_NKEYS:base + 2 * PEER_NKEYS, :]
            dsum = s1 + s2
            gate = jnp.exp(dsum + aux_ref[AUX_OFF + h:AUX_OFF + h + 1, :])
            w = w + jnp.where(dsum >= aux_ref[AUX_THR + h:AUX_THR + h + 1, :], gate, 0.0)
        return w

    def scattered_gates(gi):
        row_e = lax.broadcasted_iota(jnp.int32, (PEER_NKEYS, tb), 0) + (j * ec + gi * PEER_NKEYS)
        w = jnp.zeros((PEER_NKEYS, tb), F32)
        for s in range(PEER_SLOTS):
            w = w + jnp.where(ids_ref[s:s + 1, :] == row_e, gates_ref[s:s + 1, :], 0.0)
        return w

    def step(gates_of):
        xt = xt_ref[...]
        acts = []
        for gi in range(groups):
            rows = slice(gi * PEER_NKEYS, (gi + 1) * PEER_NKEYS)
            hid = _dot(u_ref[rows, :], xt)
            gelu = 0.5 * hid * (1.0 + lax.erf(hid * (2.0 ** -0.5)))
            acts.append((gelu * gates_of(gi)).astype(BF16))
        acc_ref[...] += _dot(vt_ref[...], jnp.concatenate(acts, axis=0))

    any_tie = jnp.max(aux_ref[AUX_TIE:AUX_TIE + 1, :]) > 0.0
    pl.when(jnp.logical_not(any_tie))(lambda: step(threshold_gates))
    pl.when(any_tie)(lambda: step(scattered_gates))

    @pl.when(j == pl.num_programs(1) - 1)
    def _():
        o_ref[...] = _layer_norm(DN_ALPHA * x_ref[...] + acc_ref[...].T, g_ref[...], b_ref[...])


def _peer(xt, x, scores, aux, ids, gates, u_all, vt_all, layer, g, b):
    t, d = x.shape
    tb, ec = PEER_COLS, PEER_ECHUNK
    ne = u_all.shape[1]
    col = lambda rows: pl.BlockSpec((rows, tb), lambda i, j: (0, i))
    return pl.pallas_call(
        _peer_kernel,
        grid=(t // tb, ne // ec),
        in_specs=[col(d),
                  pl.BlockSpec((tb, d), lambda i, j: (i, 0)),
                  col(PEER_SCORE_ROWS), col(AUX_ROWS), col(PEER_SLOTS), col(PEER_SLOTS),
                  pl.BlockSpec((None, ec, d), lambda i, j: (layer, j, 0)),
                  pl.BlockSpec((None, d, ec), lambda i, j: (layer, 0, j)),
                  pl.BlockSpec((1, d), lambda i, j: (0, 0)),
                  pl.BlockSpec((1, d), lambda i, j: (0, 0))],
        out_specs=pl.BlockSpec((tb, d), lambda i, j: (i, 0)),
        out_shape=jax.ShapeDtypeStruct((t, d), F32),
        scratch_shapes=[pltpu.VMEM((d, tb), F32)],
        compiler_params=_cparams("parallel", "arbitrary"),
        name="peer_dense",
    )(xt, x, scores, aux, ids, gates, u_all, vt_all, g, b)


def _pad_lanes(vec):
    return jnp.zeros((1, LANES), F32).at[0, :vec.shape[0]].set(vec.astype(F32))


def _gdn_weight(w):
    qw, vw, nh = GDN_QK_WIDTH, MIX_WIDTH, N_MIX_HEADS
    d = w.shape[0]
    o_b = 2 * qw + 2 * vw
    main = w[:, :o_b]
    b_cols = w[:, o_b:o_b + nh]
    a_cols = w[:, o_b + nh:o_b + 2 * nh]
    qm = w[:, o_b + 2 * nh:]
    zpad = jnp.zeros((d, LANES - nh), w.dtype)
    return jnp.concatenate([main, qm, b_cols, zpad, a_cols, zpad], axis=1).astype(BF16)


def _forward(x, mem, gdn_w_in, gdn_conv, gdn_a_log, gdn_dt_bias, gdn_onorm, sb_w_in, mem_w_kv, w_out,
             ln_mix_g, ln_mix_b, peer_w_q, peer_keys, peer_u, peer_v, ln_ffn_g, ln_ffn_b):
    t = x.shape[1]
    d = x.shape[2]
    xt = x.reshape(t, d)
    memt = mem.reshape(N_MEM, d)
    row = lambda a: a.reshape(1, -1).astype(F32)
    u_all = peer_u.astype(BF16)
    vt_all = jnp.swapaxes(peer_v, 1, 2).astype(BF16)
    for i in range(DEPTH):
        li = i // 2
        if i % 2 == 0:
            p = _matmul(xt, _gdn_weight(gdn_w_in[li]), 1024, 1280)
            qn, kn, vc, beta, gcum = _gdn_prep(p, gdn_conv[li].astype(F32),
                                               _pad_lanes(gdn_a_log[li]), _pad_lanes(gdn_dt_bias[li]))
            u, w, qg, kd, aqk, egl = _gdn_chunk(qn, kn, vc, beta, gcum)
            mix = _gdn_scan(u, w, qg, kd, aqk, egl, p, row(gdn_onorm[li]))
            qm_col = GDN_COL_QM
        else:
            p = _matmul(xt, sb_w_in[li].astype(BF16), 1024, 896)
            mix = _stick_breaking(p)
            qm_col = SB_COL_QM
        kv = _matmul(memt, mem_w_kv[i].astype(BF16), N_MEM, 512)
        mem_o = _mem_attn(p, kv, qm_col)
        x1, x1_t = _out_ln(mix, mem_o, xt, w_out[i].astype(BF16), row(ln_mix_g[i]), row(ln_mix_b[i]))
        scores, aux, ids, gates = _route(x1_t, peer_w_q[i].T.astype(BF16), peer_keys[i].astype(BF16))
        xt = _peer(x1_t, x1, scores, aux, ids, gates, u_all, vt_all, i, row(ln_ffn_g[i]), row(ln_ffn_b[i]))
    return xt.reshape(x.shape)


def kernel(x, mem, gdn_w_in, gdn_conv, gdn_a_log, gdn_dt_bias, gdn_onorm, sb_w_in, mem_w_kv, w_out,
           ln_mix_g, ln_mix_b, peer_w_q, peer_keys, peer_u, peer_v, ln_ffn_g, ln_ffn_b):
    return _forward(x, mem, gdn_w_in, gdn_conv, gdn_a_log, gdn_dt_bias, gdn_onorm, sb_w_in, mem_w_kv,
                    w_out, ln_mix_g, ln_mix_b, peer_w_q, peer_keys, peer_u, peer_v, ln_ffn_g, ln_ffn_b)
```

```python
import functools
import math

import jax
import jax.numpy as jnp
from jax import lax
from jax.experimental import pallas as pl
from jax.experimental.pallas import tpu as pltpu

F32 = jnp.float32
BF16 = jnp.bfloat16
HI = lax.Precision.HIGHEST

D_MODEL = 2048
DEPTH = 4
HEAD_DIM = 128
N_MIX_HEADS = 8
GDN_QK_HEADS = 4
MIX_WIDTH = N_MIX_HEADS * HEAD_DIM
GDN_QK_WIDTH = GDN_QK_HEADS * HEAD_DIM
N_MEM = 256
MEM_HEADS = 4
MEM_WIDTH = MEM_HEADS * HEAD_DIM
CONV_WIDTH = 4
GDN_CHUNK = 64
SB_BLOCK = 128
PEER_HEADS = 4
PEER_NKEYS = 128
PEER_EXPERTS = PEER_NKEYS * PEER_NKEYS
PEER_HALF = 128
PEER_HALF_TOPK = 16
PEER_TOPK = 8
PEER_SLOTS = PEER_HEADS * PEER_TOPK
DN_ALPHA = (2.0 * DEPTH) ** 0.25
LN_EPS = 1e-5
RMS_EPS = 1e-6
LANES = 128
SUBLANES = 8
EXP_ZERO_BELOW = -104.0

GDN_P_WIDTH = 3840
GDN_COL_Z = 2048
GDN_COL_QM = 3072
GDN_COL_B = 3584
GDN_COL_A = 3712
SB_P_WIDTH = 3 * MIX_WIDTH + MEM_WIDTH
SB_COL_QM = 3 * MIX_WIDTH

VMEM_LIMIT = 56 * 1024 * 1024


def _cparams(*sem):
    return pltpu.CompilerParams(dimension_semantics=sem, vmem_limit_bytes=VMEM_LIMIT)


def _dot(a, b):
    return jnp.dot(a, b, preferred_element_type=F32)


def _dot_nt(a, b):
    return lax.dot_general(a, b, (((1,), (1,)), ((), ())), preferred_element_type=F32)


def _dot_tn(a, b):
    return lax.dot_general(a, b, (((0,), (0,)), ((), ())), preferred_element_type=F32)


def _dot_hi(a, b):
    return jnp.dot(a, b, precision=HI, preferred_element_type=F32)


def _split2(a):
    hi = a.astype(BF16)
    lo = (a - hi.astype(F32)).astype(BF16)
    return hi, lo


def _dot3(a2, b2):
    return _dot(a2[0], b2[0]) + (_dot(a2[0], b2[1]) + _dot(a2[1], b2[0]))


def _softplus(x):
    return jnp.maximum(x, 0.0) + jnp.log1p(jnp.exp(-jnp.abs(x)))


def _silu(x):
    return x * jax.nn.sigmoid(x)


def _layer_norm(r, g, b):
    mu = jnp.mean(r, axis=-1, keepdims=True)
    d = r - mu
    var = jnp.mean(d * d, axis=-1, keepdims=True)
    return d * lax.rsqrt(var + LN_EPS) * g + b


def _mm_kernel(x_ref, w_ref, o_ref):
    o_ref[...] = _dot(x_ref[...].astype(BF16), w_ref[...]).astype(o_ref.dtype)


def _matmul(x, w, tm, tn, out_dtype=F32):
    m, k = x.shape
    n = w.shape[1]
    return pl.pallas_call(
        _mm_kernel,
        grid=(m // tm, n // tn),
        in_specs=[pl.BlockSpec((tm, k), lambda i, j: (i, 0)),
                  pl.BlockSpec((k, tn), lambda i, j: (0, j))],
        out_specs=pl.BlockSpec((tm, tn), lambda i, j: (i, j)),
        out_shape=jax.ShapeDtypeStruct((m, n), out_dtype),
        compiler_params=_cparams("parallel", "parallel"),
        name="proj_matmul",
    )(x, w)


GDN_PREP_ROWS = 256


def _gdn_prep_kernel(pc_ref, ph_ref, pb_ref, pa_ref, cw_ref, alog_ref, dtb_ref,
                     q_ref, k_ref, v_ref, beta_ref, gcum_ref, xbuf):
    tb = GDN_PREP_ROWS
    first = pl.program_id(0) == 0
    halo = ph_ref[...]
    xbuf[0:SUBLANES, :] = jnp.where(first, 0.0, halo)
    xbuf[SUBLANES:, :] = pc_ref[...]
    n_groups = (2 * GDN_QK_WIDTH + MIX_WIDTH) // LANES
    for gi in range(n_groups):
        cols = slice(gi * LANES, (gi + 1) * LANES)
        y = jnp.zeros((tb, LANES), F32)
        for j in range(CONV_WIDTH):
            r0 = SUBLANES - (CONV_WIDTH - 1) + j
            tap = xbuf[r0:r0 + tb, cols]
            y = y + tap * cw_ref[j:j + 1, cols]
        y = _silu(y)
        if gi < 2 * GDN_QK_HEADS:
            y = y * lax.rsqrt(jnp.sum(y * y, axis=-1, keepdims=True) + RMS_EPS)
        if gi < GDN_QK_HEADS:
            q_ref[:, cols] = y * (HEAD_DIM ** -0.5)
        elif gi < 2 * GDN_QK_HEADS:
            k_ref[:, (gi - GDN_QK_HEADS) * LANES:(gi - GDN_QK_HEADS + 1) * LANES] = y
        else:
            v_ref[:, (gi - 2 * GDN_QK_HEADS) * LANES:(gi - 2 * GDN_QK_HEADS + 1) * LANES] = y
    beta_ref[...] = jax.nn.sigmoid(pb_ref[...])
    g = -jnp.exp(alog_ref[...]) * _softplus(pa_ref[...] + dtb_ref[...])
    r = lax.broadcasted_iota(jnp.int32, (tb, tb), 0)
    c = lax.broadcasted_iota(jnp.int32, (tb, tb), 1)
    shift = GDN_CHUNK.bit_length() - 1
    same = lax.shift_right_logical(r, shift) == lax.shift_right_logical(c, shift)
    tri = jnp.where(same & (c <= r), 1.0, 0.0).astype(F32)
    gcum_ref[...] = _dot_hi(tri, g)


def _gdn_prep(p, conv_w, alog_pad, dtb_pad):
    t = p.shape[0]
    tb = GDN_PREP_ROWS
    cw = 2 * GDN_QK_WIDTH + MIX_WIDTH
    per8 = tb // SUBLANES
    return pl.pallas_call(
        _gdn_prep_kernel,
        grid=(t // tb,),
        in_specs=[
            pl.BlockSpec((tb, cw), lambda i: (i, 0)),
            pl.BlockSpec((SUBLANES, cw), lambda i: (jnp.maximum(i * per8 - 1, 0), 0)),
            pl.BlockSpec((tb, LANES), lambda i: (i, GDN_COL_B // LANES)),
            pl.BlockSpec((tb, LANES), lambda i: (i, GDN_COL_A // LANES)),
            pl.BlockSpec((CONV_WIDTH, cw), lambda i: (0, 0)),
            pl.BlockSpec((1, LANES), lambda i: (0, 0)),
            pl.BlockSpec((1, LANES), lambda i: (0, 0)),
        ],
        out_specs=[
            pl.BlockSpec((tb, GDN_QK_WIDTH), lambda i: (i, 0)),
            pl.BlockSpec((tb, GDN_QK_WIDTH), lambda i: (i, 0)),
            pl.BlockSpec((tb, MIX_WIDTH), lambda i: (i, 0)),
            pl.BlockSpec((tb, LANES), lambda i: (i, 0)),
            pl.BlockSpec((tb, LANES), lambda i: (i, 0)),
        ],
        out_shape=[
            jax.ShapeDtypeStruct((t, GDN_QK_WIDTH), F32),
            jax.ShapeDtypeStruct((t, GDN_QK_WIDTH), F32),
            jax.ShapeDtypeStruct((t, MIX_WIDTH), F32),
            jax.ShapeDtypeStruct((t, LANES), F32),
            jax.ShapeDtypeStruct((t, LANES), F32),
        ],
        scratch_shapes=[pltpu.VMEM((tb + SUBLANES, cw), F32)],
        compiler_params=_cparams("parallel"),
        name="gdn_prep",
    )(p, p, p, p, conv_w, alog_pad, dtb_pad)


GDN_PAIR = 2


def _gdn_chunk_kernel(q_ref, k_ref, v_ref, beta_ref, gcum_ref,
                      u_ref, w_ref, qg_ref, kd_ref, aqk_ref, egl_ref):
    c = GDN_CHUNK * GDN_PAIR
    shift = GDN_CHUNK.bit_length() - 1
    beta = beta_ref[...]
    gcum = gcum_ref[...]
    gcum_t = gcum.T
    for ck in range(GDN_PAIR):
        last = (ck + 1) * GDN_CHUNK - 1
        glast_col = gcum_t[:, last:last + 1]
        egl_ref[ck * N_MIX_HEADS:(ck + 1) * N_MIX_HEADS, :] = jnp.exp(
            jnp.broadcast_to(glast_col, (LANES, LANES))[0:N_MIX_HEADS, :])
    ri = lax.broadcasted_iota(jnp.int32, (c, c), 0)
    ci = lax.broadcasted_iota(jnp.int32, (c, c), 1)
    same = lax.shift_right_logical(ri, shift) == lax.shift_right_logical(ci, shift)
    causal = same & (ci <= ri)
    strict = same & (ci < ri)
    eye = jnp.where(ci == ri, 1.0, 0.0).astype(F32)
    row_chunk = lax.shift_right_logical(lax.broadcasted_iota(jnp.int32, (c, 1), 0), shift)
    rep = N_MIX_HEADS // GDN_QK_HEADS
    heads = range(N_MIX_HEADS)
    head_cols = [slice(h * LANES, (h + 1) * LANES) for h in heads]
    q_h = [q_ref[:, hq * LANES:(hq + 1) * LANES] for hq in range(GDN_QK_HEADS)]
    k_h = [k_ref[:, hq * LANES:(hq + 1) * LANES] for hq in range(GDN_QK_HEADS)]
    k_b = [k.astype(BF16) for k in k_h]
    qk = [_dot_nt(q.astype(BF16), kb16) for q, kb16 in zip(q_h, k_b)]
    g_col = [gcum[:, h:h + 1] for h in heads]
    b_col = [beta[:, h:h + 1] for h in heads]
    decay = [jnp.exp(jnp.where(causal, g_col[h] - gcum_t[h:h + 1, :], -jnp.inf)) for h in heads]
    eg = [jnp.exp(g) for g in g_col]
    kbeta = [k_h[h // rep] * b_col[h] for h in heads]
    kk = [_dot_nt(kbeta[h].astype(BF16), k_b[h // rep]) for h in heads]
    nm = [jnp.where(strict, -(kk[h] * decay[h]), 0.0) for h in heads]
    inv = [eye + n for n in nm]
    pw2 = [_split2(n) for n in nm]
    for _ in range(5):
        pw2 = [_split2(_dot3(p2, p2)) for p2 in pw2]
        inv = [inv[h] + _dot3(pw2[h], _split2(inv[h])) for h in heads]
    for h in heads:
        hd = head_cols[h]
        rhs = jnp.concatenate([v_ref[:, hd] * b_col[h], kbeta[h] * eg[h]], axis=1)
        sol = _dot3(_split2(inv[h]), _split2(rhs))
        u_ref[:, hd] = sol[:, :LANES]
        w_ref[:, hd] = sol[:, LANES:].astype(BF16)
        qg_ref[:, hd] = (q_h[h // rep] * eg[h]).astype(BF16)
        g_last = gcum[GDN_CHUNK - 1:GDN_CHUNK, h:h + 1]
        for ck in range(1, GDN_PAIR):
            last = (ck + 1) * GDN_CHUNK - 1
            g_last = jnp.where(row_chunk == ck, gcum[last:last + 1, h:h + 1], g_last)
        kd_ref[:, hd] = (k_h[h // rep] * jnp.exp(g_last - g_col[h])).astype(BF16)
        aqk = (qk[h // rep] * decay[h]).astype(BF16)
        for ck in range(GDN_PAIR):
            blk = slice(ck * GDN_CHUNK, (ck + 1) * GDN_CHUNK)
            aqk_ref[ck, h] = aqk[blk, blk]


def _gdn_chunk(qn, kn, vc, beta, gcum):
    t = qn.shape[0]
    c = GDN_CHUNK
    n = t // c
    rows = c * GDN_PAIR
    row = lambda w: pl.BlockSpec((rows, w), lambda i: (i, 0))
    return pl.pallas_call(
        _gdn_chunk_kernel,
        grid=(n // GDN_PAIR,),
        in_specs=[row(GDN_QK_WIDTH), row(GDN_QK_WIDTH), row(MIX_WIDTH), row(LANES), row(LANES)],
        out_specs=[row(MIX_WIDTH), row(MIX_WIDTH), row(MIX_WIDTH), row(MIX_WIDTH),
                   pl.BlockSpec((GDN_PAIR, N_MIX_HEADS, c, c), lambda i: (i, 0, 0, 0)),
                   pl.BlockSpec((GDN_PAIR * N_MIX_HEADS, LANES), lambda i: (i, 0))],
        out_shape=[
            jax.ShapeDtypeStruct((t, MIX_WIDTH), F32),
            jax.ShapeDtypeStruct((t, MIX_WIDTH), BF16),
            jax.ShapeDtypeStruct((t, MIX_WIDTH), BF16),
            jax.ShapeDtypeStruct((t, MIX_WIDTH), BF16),
            jax.ShapeDtypeStruct((n, N_MIX_HEADS, c, c), BF16),
            jax.ShapeDtypeStruct((n * N_MIX_HEADS, LANES), F32),
        ],
        compiler_params=_cparams("parallel"),
        name="gdn_chunk",
    )(qn, kn, vc, beta, gcum)


GDN_SCAN_CHUNKS = 4


def _gdn_scan_kernel(u_ref, w_ref, qg_ref, kd_ref, aqk_ref, egl_ref, z_ref, on_ref,
                     o_ref, s_ref):
    @pl.when(pl.program_id(0) == 0)
    def _():
        s_ref[...] = jnp.zeros_like(s_ref)

    on = on_ref[...]
    c = GDN_CHUNK
    heads = range(N_MIX_HEADS)
    cols = [slice(h * LANES, (h + 1) * LANES) for h in heads]
    state = [s_ref[h] for h in heads]
    for ck in range(GDN_SCAN_CHUNKS):
        rows = slice(ck * c, (ck + 1) * c)
        s_b = [s.astype(BF16) for s in state]
        ws = [_dot(w_ref[rows, cols[h]], s_b[h]) for h in heads]
        qs = [_dot(qg_ref[rows, cols[h]], s_b[h]) for h in heads]
        v_b = [(u_ref[rows, cols[h]] - ws[h]).astype(BF16) for h in heads]
        av = [_dot(aqk_ref[ck, h], v_b[h]) for h in heads]
        kv = [_dot_tn(kd_ref[rows, cols[h]], v_b[h]) for h in heads]
        egl = egl_ref[ck * N_MIX_HEADS:(ck + 1) * N_MIX_HEADS, :]
        state = [state[h] * egl[h:h + 1, :] + kv[h] for h in heads]
        for h in heads:
            o = qs[h] + av[h]
            o = o * lax.rsqrt(jnp.mean(o * o, axis=-1, keepdims=True) + RMS_EPS) * on
            o_ref[rows, cols[h]] = o * _silu(z_ref[rows, cols[h]])
    for h in heads:
        s_ref[h] = state[h]


def _gdn_scan(u, w, qg, kd, aqk, egl, p, onorm):
    t = u.shape[0]
    nck = GDN_SCAN_CHUNKS
    c = GDN_CHUNK * nck
    n = t // c
    row = lambda wd: pl.BlockSpec((c, wd), lambda i: (i, 0))
    return pl.pallas_call(
        _gdn_scan_kernel,
        grid=(n,),
        in_specs=[row(MIX_WIDTH), row(MIX_WIDTH), row(MIX_WIDTH), row(MIX_WIDTH),
                  pl.BlockSpec((nck, N_MIX_HEADS, GDN_CHUNK, GDN_CHUNK), lambda i: (i, 0, 0, 0)),
                  pl.BlockSpec((nck * N_MIX_HEADS, LANES), lambda i: (i, 0)),
                  pl.BlockSpec((c, MIX_WIDTH), lambda i: (i, GDN_COL_Z // MIX_WIDTH)),
                  pl.BlockSpec((1, LANES), lambda i: (0, 0))],
        out_specs=row(MIX_WIDTH),
        out_shape=jax.ShapeDtypeStruct((t, MIX_WIDTH), F32),
        scratch_shapes=[pltpu.VMEM((N_MIX_HEADS, HEAD_DIM, HEAD_DIM), F32)],
        compiler_params=_cparams("arbitrary"),
        name="gdn_scan",
    )(u, w, qg, kd, aqk, egl, p, onorm)


SB_SUBS = 8


def _sb_tiles(q_subs, k_ref, v_ref, kbs, c_runs, tri, keeps):
    blk = SB_BLOCK
    subs = range(len(q_subs))
    starts = [pl.multiple_of(kb * blk, blk) for kb in kbs]
    zs = [_dot_nt(q_subs[r], k_ref[pl.ds(starts[r], blk), :].astype(BF16)) * (HEAD_DIM ** -0.5)
          for r in subs]
    ts = [jnp.log1p(jnp.exp(-jnp.abs(z))) for z in zs]
    log_not = [jnp.where(keeps[r], -jnp.maximum(zs[r], 0.0) - ts[r], 0.0) for r in subs]
    log_beta = [jnp.minimum(zs[r], 0.0) - ts[r] for r in subs]
    hi = [x.astype(BF16) for x in log_not]
    rest = [log_not[r] - hi[r].astype(F32) for r in subs]
    mid = [x.astype(BF16) for x in rest]
    lo = [(rest[r] - mid[r].astype(F32)).astype(BF16) for r in subs]
    suffix = [_dot(hi[r], tri) + (_dot(mid[r], tri) + _dot(lo[r], tri)) for r in subs]
    att = [jnp.where(keeps[r], jnp.exp(log_beta[r] + suffix[r] + c_runs[r]), 0.0).astype(BF16) for r in subs]
    contrib = [_dot(att[r], v_ref[pl.ds(starts[r], blk), :].astype(BF16)) for r in subs]
    c_new = [c_runs[r] + jnp.sum(log_not[r], axis=-1, keepdims=True) for r in subs]
    return c_new, contrib


def _sb_kernel(q_ref, k_ref, v_ref, o_ref):
    blk, subs = SB_BLOCK, SB_SUBS
    qb = pl.program_id(1)
    ri = lax.broadcasted_iota(jnp.int32, (blk, blk), 0)
    ci = lax.broadcasted_iota(jnp.int32, (blk, blk), 1)
    tri = jnp.where(ri > ci, 1.0, 0.0).astype(BF16)
    q_subs = [q_ref[r * blk:(r + 1) * blk, :].astype(BF16) for r in range(subs)]

    def all_max(cs):
        m = cs[0]
        for c in cs[1:]:
            m = jnp.maximum(m, c)
        return jnp.max(m)

    cs, contrib = _sb_tiles(q_subs, k_ref, v_ref, [qb * subs + r for r in range(subs)],
                            [jnp.zeros((blk, 1), F32)] * subs, tri, [ci < ri] * subs)
    for r in range(subs):
        o_ref[r * blk:(r + 1) * blk, :] = contrib[r]

    last = qb * subs + subs - 1

    def cond(carry):
        return jnp.logical_and(carry[0] <= last, carry[1])

    def body(carry):
        n = carry[0]
        kbs = [qb * subs + r - n for r in range(subs)]
        new, contrib = _sb_tiles(q_subs, k_ref, v_ref, [jnp.maximum(kb, 0) for kb in kbs],
                                 list(carry[2:]), tri, [kb >= 0 for kb in kbs])
        for r in range(subs):
            o_ref[r * blk:(r + 1) * blk, :] += contrib[r]
        return (n + 1, all_max(new) > EXP_ZERO_BELOW, *new)

    lax.while_loop(cond, body, (jnp.int32(1), all_max(cs) > EXP_ZERO_BELOW, *cs))


def _stick_breaking(p):
    t = p.shape[0]
    blk = SB_BLOCK * SB_SUBS
    return pl.pallas_call(
        _sb_kernel,
        grid=(N_MIX_HEADS, t // blk),
        in_specs=[pl.BlockSpec((blk, HEAD_DIM), lambda h, i: (i, h)),
                  pl.BlockSpec((t, HEAD_DIM), lambda h, i: (0, N_MIX_HEADS + h)),
                  pl.BlockSpec((t, HEAD_DIM), lambda h, i: (0, 2 * N_MIX_HEADS + h))],
        out_specs=pl.BlockSpec((blk, HEAD_DIM), lambda h, i: (i, h)),
        out_shape=jax.ShapeDtypeStruct((t, MIX_WIDTH), F32),
        compiler_params=_cparams("parallel", "parallel"),
        name="stick_breaking",
    )(p, p, p)


MEM_ROWS = 512


def _mem_attn_kernel(q_ref, kv_ref, o_ref):
    for h in range(MEM_HEADS):
        hd = slice(h * LANES, (h + 1) * LANES)
        km = kv_ref[:, hd].astype(BF16)
        vm = kv_ref[:, MEM_WIDTH + h * LANES:MEM_WIDTH + (h + 1) * LANES].astype(BF16)
        s = _dot_nt(q_ref[:, hd].astype(BF16), km) * (HEAD_DIM ** -0.5)
        s = s - jnp.max(s, axis=-1, keepdims=True)
        e = jnp.exp(s)
        prob = e / jnp.sum(e, axis=-1, keepdims=True)
        o_ref[:, hd] = _dot(prob.astype(BF16), vm)


def _mem_attn(p, kv, qm_col):
    t = p.shape[0]
    tb = MEM_ROWS
    return pl.pallas_call(
        _mem_attn_kernel,
        grid=(t // tb,),
        in_specs=[pl.BlockSpec((tb, MEM_WIDTH), lambda i: (i, qm_col // MEM_WIDTH)),
                  pl.BlockSpec((N_MEM, 2 * MEM_WIDTH), lambda i: (0, 0))],
        out_specs=pl.BlockSpec((tb, MEM_WIDTH), lambda i: (i, 0)),
        out_shape=jax.ShapeDtypeStruct((t, MEM_WIDTH), F32),
        compiler_params=_cparams("parallel"),
        name="mem_attn",
    )(p, kv)


OUT_ROWS = 512


def _out_ln_kernel(mix_ref, mem_ref, x_ref, w_ref, g_ref, b_ref, o_ref, ot_ref):
    y = _dot(mix_ref[...].astype(BF16), w_ref[0:MIX_WIDTH, :])
    y = y + _dot(mem_ref[...].astype(BF16), w_ref[MIX_WIDTH:, :])
    o = _layer_norm(DN_ALPHA * x_ref[...] + y, g_ref[...], b_ref[...])
    o_ref[...] = o
    ot_ref[...] = o.T.astype(BF16)


def _out_ln(mix, mem_o, x, w_out, g, b):
    t, d = x.shape
    tb = OUT_ROWS
    return pl.pallas_call(
        _out_ln_kernel,
        grid=(t // tb,),
        in_specs=[pl.BlockSpec((tb, MIX_WIDTH), lambda i: (i, 0)),
                  pl.BlockSpec((tb, MEM_WIDTH), lambda i: (i, 0)),
                  pl.BlockSpec((tb, d), lambda i: (i, 0)),
                  pl.BlockSpec((MIX_WIDTH + MEM_WIDTH, d), lambda i: (0, 0)),
                  pl.BlockSpec((1, d), lambda i: (0, 0)),
                  pl.BlockSpec((1, d), lambda i: (0, 0))],
        out_specs=[pl.BlockSpec((tb, d), lambda i: (i, 0)),
                   pl.BlockSpec((d, tb), lambda i: (0, i))],
        out_shape=[jax.ShapeDtypeStruct((t, d), F32),
                   jax.ShapeDtypeStruct((d, t), BF16)],
        compiler_params=_cparams("parallel"),
        name="out_proj_ln",
    )(mix, mem_o, x, w_out, g, b)


ROUTE_COLS = 512
PEER_HALF_KEEP = PEER_TOPK
assert PEER_HALF_KEEP <= PEER_HALF_TOPK
PEER_NCAND = PEER_HALF_KEEP * PEER_HALF_KEEP
PEER_SCORE_ROWS = PEER_HEADS * 2 * PEER_NKEYS
AUX_ROWS = 16
AUX_THR = 0
AUX_OFF = PEER_HEADS
AUX_TIE = 2 * PEER_HEADS


def _top_rows(vals, row_idx, count, limit, emit):
    for r in range(count):
        m = jnp.max(vals, axis=0, keepdims=True)
        am = jnp.min(jnp.where(vals == m, row_idx, float(limit)), axis=0, keepdims=True)
        sel = row_idx == am
        emit(r, m, am, sel)
        vals = jnp.where(sel, -jnp.inf, vals)
    return jnp.max(vals, axis=0, keepdims=True)


def _route_kernel(xt_ref, wq_ref, keys_ref, sc_ref, aux_ref, ids_ref, gates_ref,
                  sv_ref, ix_ref, cand_ref, eid_ref):
    tb = ROUTE_COLS
    qry = _dot(wq_ref[...], xt_ref[...])
    row_k = lax.broadcasted_iota(jnp.int32, (PEER_NKEYS, tb), 0).astype(F32)
    row_c = lax.broadcasted_iota(jnp.int32, (PEER_NCAND, tb), 0).astype(F32)
    tie = jnp.zeros((1, tb), F32)
    for h in range(PEER_HEADS):
        left = []
        for half in range(2):
            r0 = (h * 2 + half) * PEER_NKEYS
            s = _dot(keys_ref[half], qry[r0:r0 + PEER_NKEYS, :].astype(BF16))
            sc_ref[r0:r0 + PEER_NKEYS, :] = s

            def emit_half(r, m, am, sel, half=half):
                sv_ref[half, r:r + 1, :] = m
                ix_ref[half, r:r + 1, :] = am

            left.append(_top_rows(s, row_k, PEER_HALF_KEEP, PEER_NKEYS, emit_half))
        sv1, sv2 = sv_ref[0], sv_ref[1]
        ix1, ix2 = ix_ref[0], ix_ref[1]
        for a in range(PEER_HALF_KEEP):
            rows = slice(a * PEER_HALF_KEEP, (a + 1) * PEER_HALF_KEEP)
            cand_ref[rows, :] = sv1[a:a + 1, :] + sv2
            eid_ref[rows, :] = ix1[a:a + 1, :] * float(PEER_NKEYS) + ix2
        eid = eid_ref[...]
        cvs, eids = [], []

        def emit_cand(r, m, am, sel):
            cvs.append(m)
            eids.append(jnp.max(jnp.where(sel, eid, -1.0), axis=0, keepdims=True))

        runner_up = _top_rows(cand_ref[...], row_c, PEER_TOPK, PEER_NCAND, emit_cand)
        thr = cvs[PEER_TOPK - 1]
        outside = jnp.maximum(left[0] + sv2[0:1, :], sv1[0:1, :] + left[1])
        tie = jnp.maximum(tie, jnp.where(jnp.maximum(outside, runner_up) >= thr, 1.0, 0.0))
        exps = [jnp.exp(cv - cvs[0]) for cv in cvs]
        denom = exps[0]
        for e in exps[1:]:
            denom = denom + e
        aux_ref[AUX_THR + h:AUX_THR + h + 1, :] = thr
        aux_ref[AUX_OFF + h:AUX_OFF + h + 1, :] = -(cvs[0] + jnp.log(denom))
        for r in range(PEER_TOPK):
            slot = h * PEER_TOPK + r
            ids_ref[slot:slot + 1, :] = eids[r].astype(jnp.int32)
            gates_ref[slot:slot + 1, :] = exps[r] / denom
    aux_ref[AUX_TIE:AUX_TIE + 1, :] = tie
    aux_ref[AUX_TIE + 1:, :] = jnp.zeros((AUX_ROWS - AUX_TIE - 1, tb), F32)


def _route(xt, wq_t, keys):
    d, t = xt.shape
    tb = ROUTE_COLS
    col = lambda rows: pl.BlockSpec((rows, tb), lambda i: (0, i))
    return pl.pallas_call(
        _route_kernel,
        grid=(t // tb,),
        in_specs=[col(d),
                  pl.BlockSpec((PEER_SCORE_ROWS, d), lambda i: (0, 0)),
                  pl.BlockSpec((2, PEER_NKEYS, PEER_HALF), lambda i: (0, 0, 0))],
        out_specs=[col(PEER_SCORE_ROWS), col(AUX_ROWS), col(PEER_SLOTS), col(PEER_SLOTS)],
        out_shape=[jax.ShapeDtypeStruct((PEER_SCORE_ROWS, t), F32),
                   jax.ShapeDtypeStruct((AUX_ROWS, t), F32),
                   jax.ShapeDtypeStruct((PEER_SLOTS, t), jnp.int32),
                   jax.ShapeDtypeStruct((PEER_SLOTS, t), F32)],
        scratch_shapes=[pltpu.VMEM((2, PEER_HALF_KEEP, tb), F32),
                        pltpu.VMEM((2, PEER_HALF_KEEP, tb), F32),
                        pltpu.VMEM((PEER_NCAND, tb), F32),
                        pltpu.VMEM((PEER_NCAND, tb), F32)],
        compiler_params=_cparams("parallel"),
        name="peer_route",
    )(xt, wq_t, keys)


PEER_COLS = 512
PEER_ECHUNK = 1024

def _peer_kernel(xt_ref, x_ref, sc_ref, aux_ref, ids_ref, gates_ref, u_ref, vt_ref, g_ref, b_ref,
                 o_ref, acc_ref):
    j = pl.program_id(1)
    tb, ec = PEER_COLS, PEER_ECHUNK
    groups = ec // PEER_NKEYS

    @pl.when(j == 0)
    def _():
        acc_ref[...] = jnp.zeros_like(acc_ref)

    def threshold_gates(gi):
        w = jnp.zeros((PEER_NKEYS, tb), F32)
        for h in range(PEER_HEADS):
            base = h * 2 * PEER_NKEYS
            s1 = sc_ref[pl.ds(base + j * groups + gi, 1), :]
            s2 = sc_ref[base + PEER_NKEYS:base + 2 * PEER_NKEYS, :]
            dsum = s1 + s2
            gate = jnp.exp(dsum + aux_ref[AUX_OFF + h:AUX_OFF + h + 1, :])
            w = w + jnp.where(dsum >= aux_ref[AUX_THR + h:AUX_THR + h + 1, :], gate, 0.0)
        return w

    def scattered_gates(gi):
        row_e = lax.broadcasted_iota(jnp.int32, (PEER_NKEYS, tb), 0) + (j * ec + gi * PEER_NKEYS)
        w = jnp.zeros((PEER_NKEYS, tb), F32)
        for s in range(PEER_SLOTS):
            w = w + jnp.where(ids_ref[s:s + 1, :] == row_e, gates_ref[s:s + 1, :], 0.0)
        return w

    def step(gates_of):
        xt = xt_ref[...]
        acts = []
        for gi in range(groups):
            rows = slice(gi * PEER_NKEYS, (gi + 1) * PEER_NKEYS)
            hid = _dot(u_ref[rows, :], xt)
            gelu = 0.5 * hid * (1.0 + lax.erf(hid * (2.0 ** -0.5)))
            acts.append((gelu * gates_of(gi)).astype(BF16))
        acc_ref[...] += _dot(vt_ref[...], jnp.concatenate(acts, axis=0))

    any_tie = jnp.max(aux_ref[AUX_TIE:AUX_TIE + 1, :]) > 0.0
    pl.when(jnp.logical_not(any_tie))(lambda: step(threshold_gates))
    pl.when(any_tie)(lambda: step(scattered_gates))

    @pl.when(j == pl.num_programs(1) - 1)
    def _():
        o_ref[...] = _layer_norm(DN_ALPHA * x_ref[...] + acc_ref[...].T, g_ref[...], b_ref[...])


def _peer(xt, x, scores, aux, ids, gates, u_all, vt_all, layer, g, b):
    t, d = x.shape
    tb, ec = PEER_COLS, PEER_ECHUNK
    ne = u_all.shape[1]
    col = lambda rows: pl.BlockSpec((rows, tb), lambda i, j: (0, i))
    return pl.pallas_call(
        _peer_kernel,
        grid=(t // tb, ne // ec),
        in_specs=[col(d),
                  pl.BlockSpec((tb, d), lambda i, j: (i, 0)),
                  col(PEER_SCORE_ROWS), col(AUX_ROWS), col(PEER_SLOTS), col(PEER_SLOTS),
                  pl.BlockSpec((None, ec, d), lambda i, j: (layer, j, 0)),
                  pl.BlockSpec((None, d, ec), lambda i, j: (layer, 0, j)),
                  pl.BlockSpec((1, d), lambda i, j: (0, 0)),
                  pl.BlockSpec((1, d), lambda i, j: (0, 0))],
        out_specs=pl.BlockSpec((tb, d), lambda i, j: (i, 0)),
        out_shape=jax.ShapeDtypeStruct((t, d), F32),
        scratch_shapes=[pltpu.VMEM((d, tb), F32)],
        compiler_params=_cparams("parallel", "arbitrary"),
        name="peer_dense",
    )(xt, x, scores, aux, ids, gates, u_all, vt_all, g, b)


def _pad_lanes(vec):
    return jnp.zeros((1, LANES), F32).at[0, :vec.shape[0]].set(vec.astype(F32))


def _gdn_weight(w):
    qw, vw, nh = GDN_QK_WIDTH, MIX_WIDTH, N_MIX_HEADS
    d = w.shape[0]
    o_b = 2 * qw + 2 * vw
    main = w[:, :o_b]
    b_cols = w[:, o_b:o_b + nh]
    a_cols = w[:, o_b + nh:o_b + 2 * nh]
    qm = w[:, o_b + 2 * nh:]
    zpad = jnp.zeros((d, LANES - nh), w.dtype)
    return jnp.concatenate([main, qm, b_cols, zpad, a_cols, zpad], axis=1).astype(BF16)


def _forward(x, mem, gdn_w_in, gdn_conv, gdn_a_log, gdn_dt_bias, gdn_onorm, sb_w_in, mem_w_kv, w_out,
             ln_mix_g, ln_mix_b, peer_w_q, peer_keys, peer_u, peer_v, ln_ffn_g, ln_ffn_b):
    t = x.shape[1]
    d = x.shape[2]
    xt = x.reshape(t, d)
    memt = mem.reshape(N_MEM, d)
    row = lambda a: a.reshape(1, -1).astype(F32)
    u_all = peer_u.astype(BF16)
    vt_all = jnp.swapaxes(peer_v, 1, 2).astype(BF16)
    for i in range(DEPTH):
        li = i // 2
        if i % 2 == 0:
            p = _matmul(xt, _gdn_weight(gdn_w_in[li]), 1024, 1280)
            qn, kn, vc, beta, gcum = _gdn_prep(p, gdn_conv[li].astype(F32),
                                               _pad_lanes(gdn_a_log[li]), _pad_lanes(gdn_dt_bias[li]))
            u, w, qg, kd, aqk, egl = _gdn_chunk(qn, kn, vc, beta, gcum)
            mix = _gdn_scan(u, w, qg, kd, aqk, egl, p, row(gdn_onorm[li]))
            qm_col = GDN_COL_QM
        else:
            p = _matmul(xt, sb_w_in[li].astype(BF16), 1024, 896, BF16)
            mix = _stick_breaking(p)
            qm_col = SB_COL_QM
        kv = _matmul(memt, mem_w_kv[i].astype(BF16), N_MEM, 512)
        mem_o = _mem_attn(p, kv, qm_col)
        x1, x1_t = _out_ln(mix, mem_o, xt, w_out[i].astype(BF16), row(ln_mix_g[i]), row(ln_mix_b[i]))
        scores, aux, ids, gates = _route(x1_t, peer_w_q[i].T.astype(BF16), peer_keys[i].astype(BF16))
        xt = _peer(x1_t, x1, scores, aux, ids, gates, u_all, vt_all, i, row(ln_ffn_g[i]), row(ln_ffn_b[i]))
    return xt.reshape(x.shape)


def kernel(x, mem, gdn_w_in, gdn_conv, gdn_a_log, gdn_dt_bias, gdn_onorm, sb_w_in, mem_w_kv, w_out,
           ln_mix_g, ln_mix_b, peer_w_q, peer_keys, peer_u, peer_v, ln_ffn_g, ln_ffn_b):
    return _forward(x, mem, gdn_w_in, gdn_conv, gdn_a_log, gdn_dt_bias, gdn_onorm, sb_w_in, mem_w_kv,
                    w_out, ln_mix_g, ln_mix_b, peer_w_q, peer_keys, peer_u, peer_v, ln_ffn_g, ln_ffn_b)
```

```python
import functools
import math

import jax
import jax.numpy as jnp
from jax import lax
from jax.experimental import pallas as pl
from jax.experimental.pallas import tpu as pltpu

F32 = jnp.float32
BF16 = jnp.bfloat16
HI = lax.Precision.HIGHEST

D_MODEL = 2048
DEPTH = 4
HEAD_DIM = 128
N_MIX_HEADS = 8
GDN_QK_HEADS = 4
MIX_WIDTH = N_MIX_HEADS * HEAD_DIM
GDN_QK_WIDTH = GDN_QK_HEADS * HEAD_DIM
N_MEM = 256
MEM_HEADS = 4
MEM_WIDTH = MEM_HEADS * HEAD_DIM
CONV_WIDTH = 4
GDN_CHUNK = 64
SB_BLOCK = 128
PEER_HEADS = 4
PEER_NKEYS = 128
PEER_EXPERTS = PEER_NKEYS * PEER_NKEYS
PEER_HALF = 128
PEER_HALF_TOPK = 16
PEER_TOPK = 8
PEER_SLOTS = PEER_HEADS * PEER_TOPK
DN_ALPHA = (2.0 * DEPTH) ** 0.25
LN_EPS = 1e-5
RMS_EPS = 1e-6
LANES = 128
SUBLANES = 8
EXP_ZERO_BELOW = -104.0

GDN_P_WIDTH = 3840
GDN_COL_Z = 2048
GDN_COL_QM = 3072
GDN_COL_B = 3584
GDN_COL_A = 3712
SB_P_WIDTH = 3 * MIX_WIDTH + MEM_WIDTH
SB_COL_QM = 3 * MIX_WIDTH

VMEM_LIMIT = 56 * 1024 * 1024


def _cparams(*sem):
    return pltpu.CompilerParams(dimension_semantics=sem, vmem_limit_bytes=VMEM_LIMIT)


def _dot(a, b):
    return jnp.dot(a, b, preferred_element_type=F32)


def _dot_nt(a, b):
    return lax.dot_general(a, b, (((1,), (1,)), ((), ())), preferred_element_type=F32)


def _dot_tn(a, b):
    return lax.dot_general(a, b, (((0,), (0,)), ((), ())), preferred_element_type=F32)


def _dot_hi(a, b):
    return jnp.dot(a, b, precision=HI, preferred_element_type=F32)


def _split2(a):
    hi = a.astype(BF16)
    lo = (a - hi.astype(F32)).astype(BF16)
    return hi, lo


def _dot3(a2, b2):
    return _dot(a2[0], b2[0]) + (_dot(a2[0], b2[1]) + _dot(a2[1], b2[0]))


def _softplus(x):
    return jnp.maximum(x, 0.0) + jnp.log1p(jnp.exp(-jnp.abs(x)))


def _silu(x):
    return x * jax.nn.sigmoid(x)


def _layer_norm(r, g, b):
    mu = jnp.mean(r, axis=-1, keepdims=True)
    d = r - mu
    var = jnp.mean(d * d, axis=-1, keepdims=True)
    return d * lax.rsqrt(var + LN_EPS) * g + b


def _mm_kernel(x_ref, w_ref, o_ref):
    o_ref[...] = _dot(x_ref[...].astype(BF16), w_ref[...]).astype(o_ref.dtype)


def _matmul(x, w, tm, tn, out_dtype=F32):
    m, k = x.shape
    n = w.shape[1]
    return pl.pallas_call(
        _mm_kernel,
        grid=(m // tm, n // tn),
        in_specs=[pl.BlockSpec((tm, k), lambda i, j: (i, 0)),
                  pl.BlockSpec((k, tn), lambda i, j: (0, j))],
        out_specs=pl.BlockSpec((tm, tn), lambda i, j: (i, j)),
        out_shape=jax.ShapeDtypeStruct((m, n), out_dtype),
        compiler_params=_cparams("parallel", "parallel"),
        name="proj_matmul",
    )(x, w)


GDN_PREP_ROWS = 256


def _gdn_prep_kernel(pc_ref, ph_ref, pb_ref, pa_ref, cw_ref, alog_ref, dtb_ref,
                     q_ref, k_ref, v_ref, beta_ref, gcum_ref, xbuf):
    tb = GDN_PREP_ROWS
    first = pl.program_id(0) == 0
    halo = ph_ref[...]
    xbuf[0:SUBLANES, :] = jnp.where(first, 0.0, halo)
    xbuf[SUBLANES:, :] = pc_ref[...]
    n_groups = (2 * GDN_QK_WIDTH + MIX_WIDTH) // LANES
    for gi in range(n_groups):
        cols = slice(gi * LANES, (gi + 1) * LANES)
        y = jnp.zeros((tb, LANES), F32)
        for j in range(CONV_WIDTH):
            r0 = SUBLANES - (CONV_WIDTH - 1) + j
            tap = xbuf[r0:r0 + tb, cols]
            y = y + tap * cw_ref[j:j + 1, cols]
        y = _silu(y)
        if gi < 2 * GDN_QK_HEADS:
            y = y * lax.rsqrt(jnp.sum(y * y, axis=-1, keepdims=True) + RMS_EPS)
        if gi < GDN_QK_HEADS:
            q_ref[:, cols] = y * (HEAD_DIM ** -0.5)
        elif gi < 2 * GDN_QK_HEADS:
            k_ref[:, (gi - GDN_QK_HEADS) * LANES:(gi - GDN_QK_HEADS + 1) * LANES] = y
        else:
            v_ref[:, (gi - 2 * GDN_QK_HEADS) * LANES:(gi - 2 * GDN_QK_HEADS + 1) * LANES] = y
    beta_ref[...] = jax.nn.sigmoid(pb_ref[...])
    g = -jnp.exp(alog_ref[...]) * _softplus(pa_ref[...] + dtb_ref[...])
    r = lax.broadcasted_iota(jnp.int32, (tb, tb), 0)
    c = lax.broadcasted_iota(jnp.int32, (tb, tb), 1)
    shift = GDN_CHUNK.bit_length() - 1
    same = lax.shift_right_logical(r, shift) == lax.shift_right_logical(c, shift)
    tri = jnp.where(same & (c <= r), 1.0, 0.0).astype(F32)
    gcum_ref[...] = _dot_hi(tri, g)


def _gdn_prep(p, conv_w, alog_pad, dtb_pad):
    t = p.shape[0]
    tb = GDN_PREP_ROWS
    cw = 2 * GDN_QK_WIDTH + MIX_WIDTH
    per8 = tb // SUBLANES
    return pl.pallas_call(
        _gdn_prep_kernel,
        grid=(t // tb,),
        in_specs=[
            pl.BlockSpec((tb, cw), lambda i: (i, 0)),
            pl.BlockSpec((SUBLANES, cw), lambda i: (jnp.maximum(i * per8 - 1, 0), 0)),
            pl.BlockSpec((tb, LANES), lambda i: (i, GDN_COL_B // LANES)),
            pl.BlockSpec((tb, LANES), lambda i: (i, GDN_COL_A // LANES)),
            pl.BlockSpec((CONV_WIDTH, cw), lambda i: (0, 0)),
            pl.BlockSpec((1, LANES), lambda i: (0, 0)),
            pl.BlockSpec((1, LANES), lambda i: (0, 0)),
        ],
        out_specs=[
            pl.BlockSpec((tb, GDN_QK_WIDTH), lambda i: (i, 0)),
            pl.BlockSpec((tb, GDN_QK_WIDTH), lambda i: (i, 0)),
            pl.BlockSpec((tb, MIX_WIDTH), lambda i: (i, 0)),
            pl.BlockSpec((tb, LANES), lambda i: (i, 0)),
            pl.BlockSpec((tb, LANES), lambda i: (i, 0)),
        ],
        out_shape=[
            jax.ShapeDtypeStruct((t, GDN_QK_WIDTH), F32),
            jax.ShapeDtypeStruct((t, GDN_QK_WIDTH), F32),
            jax.ShapeDtypeStruct((t, MIX_WIDTH), F32),
            jax.ShapeDtypeStruct((t, LANES), F32),
            jax.ShapeDtypeStruct((t, LANES), F32),
        ],
        scratch_shapes=[pltpu.VMEM((tb + SUBLANES, cw), F32)],
        compiler_params=_cparams("parallel"),
        name="gdn_prep",
    )(p, p, p, p, conv_w, alog_pad, dtb_pad)


GDN_PAIR = 2


def _gdn_chunk_kernel(q_ref, k_ref, v_ref, beta_ref, gcum_ref,
                      u_ref, w_ref, qg_ref, kd_ref, aqk_ref, egl_ref):
    c = GDN_CHUNK * GDN_PAIR
    shift = GDN_CHUNK.bit_length() - 1
    beta = beta_ref[...]
    gcum = gcum_ref[...]
    gcum_t = gcum.T
    for ck in range(GDN_PAIR):
        last = (ck + 1) * GDN_CHUNK - 1
        glast_col = gcum_t[:, last:last + 1]
        egl_ref[ck * N_MIX_HEADS:(ck + 1) * N_MIX_HEADS, :] = jnp.exp(
            jnp.broadcast_to(glast_col, (LANES, LANES))[0:N_MIX_HEADS, :])
    ri = lax.broadcasted_iota(jnp.int32, (c, c), 0)
    ci = lax.broadcasted_iota(jnp.int32, (c, c), 1)
    same = lax.shift_right_logical(ri, shift) == lax.shift_right_logical(ci, shift)
    causal = same & (ci <= ri)
    strict = same & (ci < ri)
    eye = jnp.where(ci == ri, 1.0, 0.0).astype(F32)
    row_chunk = lax.shift_right_logical(lax.broadcasted_iota(jnp.int32, (c, 1), 0), shift)
    rep = N_MIX_HEADS // GDN_QK_HEADS
    heads = range(N_MIX_HEADS)
    head_cols = [slice(h * LANES, (h + 1) * LANES) for h in heads]
    q_h = [q_ref[:, hq * LANES:(hq + 1) * LANES] for hq in range(GDN_QK_HEADS)]
    k_h = [k_ref[:, hq * LANES:(hq + 1) * LANES] for hq in range(GDN_QK_HEADS)]
    k_b = [k.astype(BF16) for k in k_h]
    qk = [_dot_nt(q.astype(BF16), kb16) for q, kb16 in zip(q_h, k_b)]
    g_col = [gcum[:, h:h + 1] for h in heads]
    b_col = [beta[:, h:h + 1] for h in heads]
    decay = [jnp.exp(jnp.where(causal, g_col[h] - gcum_t[h:h + 1, :], -jnp.inf)) for h in heads]
    eg = [jnp.exp(g) for g in g_col]
    kbeta = [k_h[h // rep] * b_col[h] for h in heads]
    kk = [_dot_nt(kbeta[h].astype(BF16), k_b[h // rep]) for h in heads]
    nm = [jnp.where(strict, -(kk[h] * decay[h]), 0.0) for h in heads]
    inv = [eye + n for n in nm]
    pw2 = [_split2(n) for n in nm]
    for _ in range(5):
        pw2 = [_split2(_dot3(p2, p2)) for p2 in pw2]
        inv = [inv[h] + _dot3(pw2[h], _split2(inv[h])) for h in heads]
    for h in heads:
        hd = head_cols[h]
        rhs = jnp.concatenate([v_ref[:, hd] * b_col[h], kbeta[h] * eg[h]], axis=1)
        sol = _dot3(_split2(inv[h]), _split2(rhs))
        u_ref[:, hd] = sol[:, :LANES]
        w_ref[:, hd] = sol[:, LANES:].astype(BF16)
        qg_ref[:, hd] = (q_h[h // rep] * eg[h]).astype(BF16)
        g_last = gcum[GDN_CHUNK - 1:GDN_CHUNK, h:h + 1]
        for ck in range(1, GDN_PAIR):
            last = (ck + 1) * GDN_CHUNK - 1
            g_last = jnp.where(row_chunk == ck, gcum[last:last + 1, h:h + 1], g_last)
        kd_ref[:, hd] = (k_h[h // rep] * jnp.exp(g_last - g_col[h])).astype(BF16)
        aqk = (qk[h // rep] * decay[h]).astype(BF16)
        for ck in range(GDN_PAIR):
            blk = slice(ck * GDN_CHUNK, (ck + 1) * GDN_CHUNK)
            aqk_ref[ck, h] = aqk[blk, blk]


def _gdn_chunk(qn, kn, vc, beta, gcum):
    t = qn.shape[0]
    c = GDN_CHUNK
    n = t // c
    rows = c * GDN_PAIR
    row = lambda w: pl.BlockSpec((rows, w), lambda i: (i, 0))
    return pl.pallas_call(
        _gdn_chunk_kernel,
        grid=(n // GDN_PAIR,),
        in_specs=[row(GDN_QK_WIDTH), row(GDN_QK_WIDTH), row(MIX_WIDTH), row(LANES), row(LANES)],
        out_specs=[row(MIX_WIDTH), row(MIX_WIDTH), row(MIX_WIDTH), row(MIX_WIDTH),
                   pl.BlockSpec((GDN_PAIR, N_MIX_HEADS, c, c), lambda i: (i, 0, 0, 0)),
                   pl.BlockSpec((GDN_PAIR * N_MIX_HEADS, LANES), lambda i: (i, 0))],
        out_shape=[
            jax.ShapeDtypeStruct((t, MIX_WIDTH), F32),
            jax.ShapeDtypeStruct((t, MIX_WIDTH), BF16),
            jax.ShapeDtypeStruct((t, MIX_WIDTH), BF16),
            jax.ShapeDtypeStruct((t, MIX_WIDTH), BF16),
            jax.ShapeDtypeStruct((n, N_MIX_HEADS, c, c), BF16),
            jax.ShapeDtypeStruct((n * N_MIX_HEADS, LANES), F32),
        ],
        compiler_params=_cparams("parallel"),
        name="gdn_chunk",
    )(qn, kn, vc, beta, gcum)


GDN_SCAN_CHUNKS = 8


def _gdn_scan_kernel(u_ref, w_ref, qg_ref, kd_ref, aqk_ref, egl_ref, z_ref, on_ref,
                     o_ref, s_ref):
    @pl.when(pl.program_id(0) == 0)
    def _():
        s_ref[...] = jnp.zeros_like(s_ref)

    on = on_ref[...]
    c = GDN_CHUNK
    heads = range(N_MIX_HEADS)
    cols = [slice(h * LANES, (h + 1) * LANES) for h in heads]
    state = [s_ref[h] for h in heads]
    for ck in range(GDN_SCAN_CHUNKS):
        rows = slice(ck * c, (ck + 1) * c)
        s_b = [s.astype(BF16) for s in state]
        ws = [_dot(w_ref[rows, cols[h]], s_b[h]) for h in heads]
        qs = [_dot(qg_ref[rows, cols[h]], s_b[h]) for h in heads]
        v_b = [(u_ref[rows, cols[h]] - ws[h]).astype(BF16) for h in heads]
        av = [_dot(aqk_ref[ck, h], v_b[h]) for h in heads]
        kv = [_dot_tn(kd_ref[rows, cols[h]], v_b[h]) for h in heads]
        egl = egl_ref[ck * N_MIX_HEADS:(ck + 1) * N_MIX_HEADS, :]
        state = [state[h] * egl[h:h + 1, :] + kv[h] for h in heads]
        for h in heads:
            o = qs[h] + av[h]
            o = o * lax.rsqrt(jnp.mean(o * o, axis=-1, keepdims=True) + RMS_EPS) * on
            o_ref[rows, cols[h]] = o * _silu(z_ref[rows, cols[h]])
    for h in heads:
        s_ref[h] = state[h]


def _gdn_scan(u, w, qg, kd, aqk, egl, p, onorm):
    t = u.shape[0]
    nck = GDN_SCAN_CHUNKS
    c = GDN_CHUNK * nck
    n = t // c
    row = lambda wd: pl.BlockSpec((c, wd), lambda i: (i, 0))
    return pl.pallas_call(
        _gdn_scan_kernel,
        grid=(n,),
        in_specs=[row(MIX_WIDTH), row(MIX_WIDTH), row(MIX_WIDTH), row(MIX_WIDTH),
                  pl.BlockSpec((nck, N_MIX_HEADS, GDN_CHUNK, GDN_CHUNK), lambda i: (i, 0, 0, 0)),
                  pl.BlockSpec((nck * N_MIX_HEADS, LANES), lambda i: (i, 0)),
                  pl.BlockSpec((c, MIX_WIDTH), lambda i: (i, GDN_COL_Z // MIX_WIDTH)),
                  pl.BlockSpec((1, LANES), lambda i: (0, 0))],
        out_specs=row(MIX_WIDTH),
        out_shape=jax.ShapeDtypeStruct((t, MIX_WIDTH), F32),
        scratch_shapes=[pltpu.VMEM((N_MIX_HEADS, HEAD_DIM, HEAD_DIM), F32)],
        compiler_params=_cparams("arbitrary"),
        name="gdn_scan",
    )(u, w, qg, kd, aqk, egl, p, onorm)


SB_SUBS = 8


def _sb_tiles(q_subs, k_ref, v_ref, kbs, c_runs, tri, keeps):
    blk = SB_BLOCK
    subs = range(len(q_subs))
    starts = [pl.multiple_of(kb * blk, blk) for kb in kbs]
    zs = [_dot_nt(q_subs[r], k_ref[pl.ds(starts[r], blk), :].astype(BF16)) * (HEAD_DIM ** -0.5)
          for r in subs]
    ts = [jnp.log1p(jnp.exp(-jnp.abs(z))) for z in zs]
    log_not = [jnp.where(keeps[r], -jnp.maximum(zs[r], 0.0) - ts[r], 0.0) for r in subs]
    log_beta = [jnp.minimum(zs[r], 0.0) - ts[r] for r in subs]
    hi = [x.astype(BF16) for x in log_not]
    rest = [log_not[r] - hi[r].astype(F32) for r in subs]
    mid = [x.astype(BF16) for x in rest]
    lo = [(rest[r] - mid[r].astype(F32)).astype(BF16) for r in subs]
    suffix = [_dot(hi[r], tri) + (_dot(mid[r], tri) + _dot(lo[r], tri)) for r in subs]
    att = [jnp.where(keeps[r], jnp.exp(log_beta[r] + suffix[r] + c_runs[r]), 0.0).astype(BF16) for r in subs]
    contrib = [_dot(att[r], v_ref[pl.ds(starts[r], blk), :].astype(BF16)) for r in subs]
    c_new = [c_runs[r] + jnp.sum(log_not[r], axis=-1, keepdims=True) for r in subs]
    return c_new, contrib


def _sb_kernel(q_ref, k_ref, v_ref, o_ref):
    blk, subs = SB_BLOCK, SB_SUBS
    qb = pl.program_id(1)
    ri = lax.broadcasted_iota(jnp.int32, (blk, blk), 0)
    ci = lax.broadcasted_iota(jnp.int32, (blk, blk), 1)
    tri = jnp.where(ri > ci, 1.0, 0.0).astype(BF16)
    q_subs = [q_ref[r * blk:(r + 1) * blk, :].astype(BF16) for r in range(subs)]

    def all_max(cs):
        m = cs[0]
        for c in cs[1:]:
            m = jnp.maximum(m, c)
        return jnp.max(m)

    cs, contrib = _sb_tiles(q_subs, k_ref, v_ref, [qb * subs + r for r in range(subs)],
                            [jnp.zeros((blk, 1), F32)] * subs, tri, [ci < ri] * subs)
    for r in range(subs):
        o_ref[r * blk:(r + 1) * blk, :] = contrib[r]

    last = qb * subs + subs - 1

    def cond(carry):
        return jnp.logical_and(carry[0] <= last, carry[1])

    def body(carry):
        n = carry[0]
        kbs = [qb * subs + r - n for r in range(subs)]
        new, contrib = _sb_tiles(q_subs, k_ref, v_ref, [jnp.maximum(kb, 0) for kb in kbs],
                                 list(carry[2:]), tri, [kb >= 0 for kb in kbs])
        for r in range(subs):
            o_ref[r * blk:(r + 1) * blk, :] += contrib[r]
        return (n + 1, all_max(new) > EXP_ZERO_BELOW, *new)

    lax.while_loop(cond, body, (jnp.int32(1), all_max(cs) > EXP_ZERO_BELOW, *cs))


def _stick_breaking(p):
    t = p.shape[0]
    blk = SB_BLOCK * SB_SUBS
    return pl.pallas_call(
        _sb_kernel,
        grid=(N_MIX_HEADS, t // blk),
        in_specs=[pl.BlockSpec((blk, HEAD_DIM), lambda h, i: (i, h)),
                  pl.BlockSpec((t, HEAD_DIM), lambda h, i: (0, N_MIX_HEADS + h)),
                  pl.BlockSpec((t, HEAD_DIM), lambda h, i: (0, 2 * N_MIX_HEADS + h))],
        out_specs=pl.BlockSpec((blk, HEAD_DIM), lambda h, i: (i, h)),
        out_shape=jax.ShapeDtypeStruct((t, MIX_WIDTH), F32),
        compiler_params=_cparams("parallel", "parallel"),
        name="stick_breaking",
    )(p, p, p)


OUT_ROWS = 512


def _mem_attention(q_ref, kv_ref):
    heads = []
    for h in range(MEM_HEADS):
        hd = slice(h * LANES, (h + 1) * LANES)
        km = kv_ref[:, hd].astype(BF16)
        vm = kv_ref[:, MEM_WIDTH + h * LANES:MEM_WIDTH + (h + 1) * LANES].astype(BF16)
        s = _dot_nt(q_ref[:, hd].astype(BF16), km) * (HEAD_DIM ** -0.5)
        s = s - jnp.max(s, axis=-1, keepdims=True)
        e = jnp.exp(s)
        prob = e / jnp.sum(e, axis=-1, keepdims=True)
        heads.append(_dot(prob.astype(BF16), vm).astype(BF16))
    return jnp.concatenate(heads, axis=1)


def _out_ln_kernel(mix_ref, q_ref, kv_ref, x_ref, w_ref, g_ref, b_ref, o_ref, ot_ref):
    y = _dot(mix_ref[...].astype(BF16), w_ref[0:MIX_WIDTH, :])
    y = y + _dot(_mem_attention(q_ref, kv_ref), w_ref[MIX_WIDTH:, :])
    o = _layer_norm(DN_ALPHA * x_ref[...] + y, g_ref[...], b_ref[...])
    o_ref[...] = o
    ot_ref[...] = o.T.astype(BF16)


def _out_ln(mix, p, qm_col, kv, x, w_out, g, b):
    t, d = x.shape
    tb = OUT_ROWS
    return pl.pallas_call(
        _out_ln_kernel,
        grid=(t // tb,),
        in_specs=[pl.BlockSpec((tb, MIX_WIDTH), lambda i: (i, 0)),
                  pl.BlockSpec((tb, MEM_WIDTH), lambda i: (i, qm_col // MEM_WIDTH)),
                  pl.BlockSpec((N_MEM, 2 * MEM_WIDTH), lambda i: (0, 0)),
                  pl.BlockSpec((tb, d), lambda i: (i, 0)),
                  pl.BlockSpec((MIX_WIDTH + MEM_WIDTH, d), lambda i: (0, 0)),
                  pl.BlockSpec((1, d), lambda i: (0, 0)),
                  pl.BlockSpec((1, d), lambda i: (0, 0))],
        out_specs=[pl.BlockSpec((tb, d), lambda i: (i, 0)),
                   pl.BlockSpec((d, tb), lambda i: (0, i))],
        out_shape=[jax.ShapeDtypeStruct((t, d), F32),
                   jax.ShapeDtypeStruct((d, t), BF16)],
        compiler_params=_cparams("parallel"),
        name="out_proj_ln",
    )(mix, p, kv, x, w_out, g, b)


ROUTE_COLS = 512
PEER_HALF_KEEP = PEER_TOPK
assert PEER_HALF_KEEP <= PEER_HALF_TOPK
PEER_NCAND = PEER_HALF_KEEP * PEER_HALF_KEEP
PEER_SCORE_ROWS = PEER_HEADS * 2 * PEER_NKEYS
AUX_ROWS = 16
AUX_THR = 0
AUX_OFF = PEER_HEADS
AUX_TIE = 2 * PEER_HEADS


def _top_rows(vals, row_idx, count, limit, emit):
    for r in range(count):
        m = jnp.max(vals, axis=0, keepdims=True)
        am = jnp.min(jnp.where(vals == m, row_idx, float(limit)), axis=0, keepdims=True)
        sel = row_idx == am
        emit(r, m, am, sel)
        vals = jnp.where(sel, -jnp.inf, vals)
    return jnp.max(vals, axis=0, keepdims=True)


def _route_kernel(xt_ref, wq_ref, keys_ref, sc_ref, aux_ref, ids_ref, gates_ref,
                  sv_ref, ix_ref, cand_ref, eid_ref):
    tb = ROUTE_COLS
    qry = _dot(wq_ref[...], xt_ref[...])
    row_k = lax.broadcasted_iota(jnp.int32, (PEER_NKEYS, tb), 0).astype(F32)
    row_c = lax.broadcasted_iota(jnp.int32, (PEER_NCAND, tb), 0).astype(F32)
    tie = jnp.zeros((1, tb), F32)
    for h in range(PEER_HEADS):
        left = []
        for half in range(2):
            r0 = (h * 2 + half) * PEER_NKEYS
            s = _dot(keys_ref[half], qry[r0:r0 + PEER_NKEYS, :].astype(BF16))
            sc_ref[r0:r0 + PEER_NKEYS, :] = s

            def emit_half(r, m, am, sel, half=half):
                sv_ref[half, r:r + 1, :] = m
                ix_ref[half, r:r + 1, :] = am

            left.append(_top_rows(s, row_k, PEER_HALF_KEEP, PEER_NKEYS, emit_half))
        sv1, sv2 = sv_ref[0], sv_ref[1]
        ix1, ix2 = ix_ref[0], ix_ref[1]
        for a in range(PEER_HALF_KEEP):
            rows = slice(a * PEER_HALF_KEEP, (a + 1) * PEER_HALF_KEEP)
            cand_ref[rows, :] = sv1[a:a + 1, :] + sv2
            eid_ref[rows, :] = ix1[a:a + 1, :] * float(PEER_NKEYS) + ix2
        eid = eid_ref[...]
        cvs, eids = [], []

        def emit_cand(r, m, am, sel):
            cvs.append(m)
            eids.append(jnp.max(jnp.where(sel, eid, -1.0), axis=0, keepdims=True))

        runner_up = _top_rows(cand_ref[...], row_c, PEER_TOPK, PEER_NCAND, emit_cand)
        thr = cvs[PEER_TOPK - 1]
        outside = jnp.maximum(left[0] + sv2[0:1, :], sv1[0:1, :] + left[1])
        tie = jnp.maximum(tie, jnp.where(jnp.maximum(outside, runner_up) >= thr, 1.0, 0.0))
        exps = [jnp.exp(cv - cvs[0]) for cv in cvs]
        denom = exps[0]
        for e in exps[1:]:
            denom = denom + e
        aux_ref[AUX_THR + h:AUX_THR + h + 1, :] = thr
        aux_ref[AUX_OFF + h:AUX_OFF + h + 1, :] = -(cvs[0] + jnp.log(denom))
        for r in range(PEER_TOPK):
            slot = h * PEER_TOPK + r
            ids_ref[slot:slot + 1, :] = eids[r].astype(jnp.int32)
            gates_ref[slot:slot + 1, :] = exps[r] / denom
    aux_ref[AUX_TIE:AUX_TIE + 1, :] = tie
    aux_ref[AUX_TIE + 1:, :] = jnp.zeros((AUX_ROWS - AUX_TIE - 1, tb), F32)


def _route(xt, wq_t, keys):
    d, t = xt.shape
    tb = ROUTE_COLS
    col = lambda rows: pl.BlockSpec((rows, tb), lambda i: (0, i))
    return pl.pallas_call(
        _route_kernel,
        grid=(t // tb,),
        in_specs=[col(d),
                  pl.BlockSpec((PEER_SCORE_ROWS, d), lambda i: (0, 0)),
                  pl.BlockSpec((2, PEER_NKEYS, PEER_HALF), lambda i: (0, 0, 0))],
        out_specs=[col(PEER_SCORE_ROWS), col(AUX_ROWS), col(PEER_SLOTS), col(PEER_SLOTS)],
        out_shape=[jax.ShapeDtypeStruct((PEER_SCORE_ROWS, t), F32),
                   jax.ShapeDtypeStruct((AUX_ROWS, t), F32),
                   jax.ShapeDtypeStruct((PEER_SLOTS, t), jnp.int32),
                   jax.ShapeDtypeStruct((PEER_SLOTS, t), F32)],
        scratch_shapes=[pltpu.VMEM((2, PEER_HALF_KEEP, tb), F32),
                        pltpu.VMEM((2, PEER_HALF_KEEP, tb), F32),
                        pltpu.VMEM((PEER_NCAND, tb), F32),
                        pltpu.VMEM((PEER_NCAND, tb), F32)],
        compiler_params=_cparams("parallel"),
        name="peer_route",
    )(xt, wq_t, keys)


PEER_COLS = 512
PEER_ECHUNK = 1024

def _peer_kernel(xt_ref, x_ref, sc_ref, aux_ref, ids_ref, gates_ref, u_ref, vt_ref, g_ref, b_ref,
                 o_ref, acc_ref):
    j = pl.program_id(1)
    tb, ec = PEER_COLS, PEER_ECHUNK
    groups = ec // PEER_NKEYS

    @pl.when(j == 0)
    def _():
        acc_ref[...] = jnp.zeros_like(acc_ref)

    def threshold_gates(gi):
        w = jnp.zeros((PEER_NKEYS, tb), F32)
        for h in range(PEER_HEADS):
            base = h * 2 * PEER_NKEYS
            s1 = sc_ref[pl.ds(base + j * groups + gi, 1), :]
            s2 = sc_ref[base + PEER_NKEYS:base + 2 * PEER_NKEYS, :]
            dsum = s1 + s2
            gate = jnp.exp(dsum + aux_ref[AUX_OFF + h:AUX_OFF + h + 1, :])
            w = w + jnp.where(dsum >= aux_ref[AUX_THR + h:AUX_THR + h + 1, :], gate, 0.0)
        return w

    def scattered_gates(gi):
        row_e = lax.broadcasted_iota(jnp.int32, (PEER_NKEYS, tb), 0) + (j * ec + gi * PEER_NKEYS)
        w = jnp.zeros((PEER_NKEYS, tb), F32)
        for s in range(PEER_SLOTS):
            w = w + jnp.where(ids_ref[s:s + 1, :] == row_e, gates_ref[s:s + 1, :], 0.0)
        return w

    def step(gates_of):
        xt = xt_ref[...]
        acts = []
        for gi in range(groups):
            rows = slice(gi * PEER_NKEYS, (gi + 1) * PEER_NKEYS)
            hid = _dot(u_ref[rows, :], xt)
            gelu = 0.5 * hid * (1.0 + lax.erf(hid * (2.0 ** -0.5)))
            acts.append((gelu * gates_of(gi)).astype(BF16))
        acc_ref[...] += _dot(vt_ref[...], jnp.concatenate(acts, axis=0))

    any_tie = jnp.max(aux_ref[AUX_TIE:AUX_TIE + 1, :]) > 0.0
    pl.when(jnp.logical_not(any_tie))(lambda: step(threshold_gates))
    pl.when(any_tie)(lambda: step(scattered_gates))

    @pl.when(j == pl.num_programs(1) - 1)
    def _():
        o_ref[...] = _layer_norm(DN_ALPHA * x_ref[...] + acc_ref[...].T, g_ref[...], b_ref[...])


def _peer(xt, x, scores, aux, ids, gates, u_all, vt_all, layer, g, b):
    t, d = x.shape
    tb, ec = PEER_COLS, PEER_ECHUNK
    ne = u_all.shape[1]
    col = lambda rows: pl.BlockSpec((rows, tb), lambda i, j: (0, i))
    return pl.pallas_call(
        _peer_kernel,
        grid=(t // tb, ne // ec),
        in_specs=[col(d),
                  pl.BlockSpec((tb, d), lambda i, j: (i, 0)),
                  col(PEER_SCORE_ROWS), col(AUX_ROWS), col(PEER_SLOTS), col(PEER_SLOTS),
                  pl.BlockSpec((None, ec, d), lambda i, j: (layer, j, 0)),
                  pl.BlockSpec((None, d, ec), lambda i, j: (layer, 0, j)),
                  pl.BlockSpec((1, d), lambda i, j: (0, 0)),
                  pl.BlockSpec((1, d), lambda i, j: (0, 0))],
        out_specs=pl.BlockSpec((tb, d), lambda i, j: (i, 0)),
        out_shape=jax.ShapeDtypeStruct((t, d), F32),
        scratch_shapes=[pltpu.VMEM((d, tb), F32)],
        compiler_params=_cparams("parallel", "arbitrary"),
        name="peer_dense",
    )(xt, x, scores, aux, ids, gates, u_all, vt_all, g, b)


def _pad_lanes(vec):
    return jnp.zeros((1, LANES), F32).at[0, :vec.shape[0]].set(vec.astype(F32))


def _gdn_weight(w):
    qw, vw, nh = GDN_QK_WIDTH, MIX_WIDTH, N_MIX_HEADS
    d = w.shape[0]
    o_b = 2 * qw + 2 * vw
    main = w[:, :o_b]
    b_cols = w[:, o_b:o_b + nh]
    a_cols = w[:, o_b + nh:o_b + 2 * nh]
    qm = w[:, o_b + 2 * nh:]
    zpad = jnp.zeros((d, LANES - nh), w.dtype)
    return jnp.concatenate([main, qm, b_cols, zpad, a_cols, zpad], axis=1).astype(BF16)


def _forward(x, mem, gdn_w_in, gdn_conv, gdn_a_log, gdn_dt_bias, gdn_onorm, sb_w_in, mem_w_kv, w_out,
             ln_mix_g, ln_mix_b, peer_w_q, peer_keys, peer_u, peer_v, ln_ffn_g, ln_ffn_b):
    t = x.shape[1]
    d = x.shape[2]
    xt = x.reshape(t, d)
    memt = mem.reshape(N_MEM, d)
    row = lambda a: a.reshape(1, -1).astype(F32)
    u_all = peer_u.astype(BF16)
    vt_all = jnp.swapaxes(peer_v, 1, 2).astype(BF16)
    for i in range(DEPTH):
        li = i // 2
        if i % 2 == 0:
            p = _matmul(xt, _gdn_weight(gdn_w_in[li]), 1024, 1920)
            qn, kn, vc, beta, gcum = _gdn_prep(p, gdn_conv[li].astype(F32),
                                               _pad_lanes(gdn_a_log[li]), _pad_lanes(gdn_dt_bias[li]))
            u, w, qg, kd, aqk, egl = _gdn_chunk(qn, kn, vc, beta, gcum)
            mix = _gdn_scan(u, w, qg, kd, aqk, egl, p, row(gdn_onorm[li]))
            qm_col = GDN_COL_QM
        else:
            p = _matmul(xt, sb_w_in[li].astype(BF16), 1024, 1792, BF16)
            mix = _stick_breaking(p)
            qm_col = SB_COL_QM
        kv = _matmul(memt, mem_w_kv[i].astype(BF16), N_MEM, 512)
        x1, x1_t = _out_ln(mix, p, qm_col, kv, xt, w_out[i].astype(BF16), row(ln_mix_g[i]), row(ln_mix_b[i]))
        scores, aux, ids, gates = _route(x1_t, peer_w_q[i].T.astype(BF16), peer_keys[i].astype(BF16))
        xt = _peer(x1_t, x1, scores, aux, ids, gates, u_all, vt_all, i, row(ln_ffn_g[i]), row(ln_ffn_b[i]))
    return xt.reshape(x.shape)


def kernel(x, mem, gdn_w_in, gdn_conv, gdn_a_log, gdn_dt_bias, gdn_onorm, sb_w_in, mem_w_kv, w_out,
           ln_mix_g, ln_mix_b, peer_w_q, peer_keys, peer_u, peer_v, ln_ffn_g, ln_ffn_b):
    return _forward(x, mem, gdn_w_in, gdn_conv, gdn_a_log, gdn_dt_bias, gdn_onorm, sb_w_in, mem_w_kv,
                    w_out, ln_mix_g, ln_mix_b, peer_w_q, peer_keys, peer_u, peer_v, ln_ffn_g, ln_ffn_b)
```

```python
import functools
import math

import jax
import jax.numpy as jnp
from jax import lax
from jax.experimental import pallas as pl
from jax.experimental.pallas import tpu as pltpu

F32 = jnp.float32
BF16 = jnp.bfloat16
HI = lax.Precision.HIGHEST

D_MODEL = 2048
DEPTH = 4
HEAD_DIM = 128
N_MIX_HEADS = 8
GDN_QK_HEADS = 4
MIX_WIDTH = N_MIX_HEADS * HEAD_DIM
GDN_QK_WIDTH = GDN_QK_HEADS * HEAD_DIM
N_MEM = 256
MEM_HEADS = 4
MEM_WIDTH = MEM_HEADS * HEAD_DIM
CONV_WIDTH = 4
GDN_CHUNK = 64
SB_BLOCK = 128
PEER_HEADS = 4
PEER_NKEYS = 128
PEER_EXPERTS = PEER_NKEYS * PEER_NKEYS
PEER_HALF = 128
PEER_HALF_TOPK = 16
PEER_TOPK = 8
PEER_SLOTS = PEER_HEADS * PEER_TOPK
DN_ALPHA = (2.0 * DEPTH) ** 0.25
LN_EPS = 1e-5
RMS_EPS = 1e-6
LANES = 128
SUBLANES = 8
EXP_ZERO_BELOW = -104.0

GDN_P_WIDTH = 3840
GDN_COL_Z = 2048
GDN_COL_QM = 3072
GDN_COL_B = 3584
GDN_COL_A = 3712
SB_P_WIDTH = 3 * MIX_WIDTH + MEM_WIDTH
SB_COL_QM = 3 * MIX_WIDTH

VMEM_LIMIT = 56 * 1024 * 1024


def _cparams(*sem):
    return pltpu.CompilerParams(dimension_semantics=sem, vmem_limit_bytes=VMEM_LIMIT)


def _dot(a, b):
    return jnp.dot(a, b, preferred_element_type=F32)


def _dot_nt(a, b):
    return lax.dot_general(a, b, (((1,), (1,)), ((), ())), preferred_element_type=F32)


def _dot_tn(a, b):
    return lax.dot_general(a, b, (((0,), (0,)), ((), ())), preferred_element_type=F32)


def _dot_hi(a, b):
    return jnp.dot(a, b, precision=HI, preferred_element_type=F32)


def _split2(a):
    hi = a.astype(BF16)
    lo = (a - hi.astype(F32)).astype(BF16)
    return hi, lo


def _dot3(a2, b2):
    return _dot(a2[0], b2[0]) + (_dot(a2[0], b2[1]) + _dot(a2[1], b2[0]))


def _softplus(x):
    return jnp.maximum(x, 0.0) + jnp.log1p(jnp.exp(-jnp.abs(x)))


def _silu(x):
    return x * jax.nn.sigmoid(x)


def _layer_norm(r, g, b):
    mu = jnp.mean(r, axis=-1, keepdims=True)
    d = r - mu
    var = jnp.mean(d * d, axis=-1, keepdims=True)
    return d * lax.rsqrt(var + LN_EPS) * g + b


def _mm_kernel(x_ref, w_ref, o_ref):
    o_ref[...] = _dot(x_ref[...].astype(BF16), w_ref[...]).astype(o_ref.dtype)


def _matmul(x, w, tm, tn, out_dtype=F32):
    m, k = x.shape
    n = w.shape[1]
    return pl.pallas_call(
        _mm_kernel,
        grid=(m // tm, n // tn),
        in_specs=[pl.BlockSpec((tm, k), lambda i, j: (i, 0)),
                  pl.BlockSpec((k, tn), lambda i, j: (0, j))],
        out_specs=pl.BlockSpec((tm, tn), lambda i, j: (i, j)),
        out_shape=jax.ShapeDtypeStruct((m, n), out_dtype),
        compiler_params=_cparams("parallel", "parallel"),
        name="proj_matmul",
    )(x, w)


GDN_PREP_ROWS = 256


def _gdn_prep_kernel(pc_ref, ph_ref, pb_ref, pa_ref, cw_ref, alog_ref, dtb_ref,
                     q_ref, k_ref, v_ref, beta_ref, gcum_ref, xbuf):
    tb = GDN_PREP_ROWS
    first = pl.program_id(0) == 0
    halo = ph_ref[...]
    xbuf[0:SUBLANES, :] = jnp.where(first, 0.0, halo)
    xbuf[SUBLANES:, :] = pc_ref[...]
    n_groups = (2 * GDN_QK_WIDTH + MIX_WIDTH) // LANES
    for gi in range(n_groups):
        cols = slice(gi * LANES, (gi + 1) * LANES)
        y = jnp.zeros((tb, LANES), F32)
        for j in range(CONV_WIDTH):
            r0 = SUBLANES - (CONV_WIDTH - 1) + j
            tap = xbuf[r0:r0 + tb, cols]
            y = y + tap * cw_ref[j:j + 1, cols]
        y = _silu(y)
        if gi < 2 * GDN_QK_HEADS:
            y = y * lax.rsqrt(jnp.sum(y * y, axis=-1, keepdims=True) + RMS_EPS)
        if gi < GDN_QK_HEADS:
            q_ref[:, cols] = y * (HEAD_DIM ** -0.5)
        elif gi < 2 * GDN_QK_HEADS:
            k_ref[:, (gi - GDN_QK_HEADS) * LANES:(gi - GDN_QK_HEADS + 1) * LANES] = y
        else:
            v_ref[:, (gi - 2 * GDN_QK_HEADS) * LANES:(gi - 2 * GDN_QK_HEADS + 1) * LANES] = y
    beta_ref[...] = jax.nn.sigmoid(pb_ref[...])
    g = -jnp.exp(alog_ref[...]) * _softplus(pa_ref[...] + dtb_ref[...])
    r = lax.broadcasted_iota(jnp.int32, (tb, tb), 0)
    c = lax.broadcasted_iota(jnp.int32, (tb, tb), 1)
    shift = GDN_CHUNK.bit_length() - 1
    same = lax.shift_right_logical(r, shift) == lax.shift_right_logical(c, shift)
    tri = jnp.where(same & (c <= r), 1.0, 0.0).astype(F32)
    gcum_ref[...] = _dot_hi(tri, g)


def _gdn_prep(p, conv_w, alog_pad, dtb_pad):
    t = p.shape[0]
    tb = GDN_PREP_ROWS
    cw = 2 * GDN_QK_WIDTH + MIX_WIDTH
    per8 = tb // SUBLANES
    return pl.pallas_call(
        _gdn_prep_kernel,
        grid=(t // tb,),
        in_specs=[
            pl.BlockSpec((tb, cw), lambda i: (i, 0)),
            pl.BlockSpec((SUBLANES, cw), lambda i: (jnp.maximum(i * per8 - 1, 0), 0)),
            pl.BlockSpec((tb, LANES), lambda i: (i, GDN_COL_B // LANES)),
            pl.BlockSpec((tb, LANES), lambda i: (i, GDN_COL_A // LANES)),
            pl.BlockSpec((CONV_WIDTH, cw), lambda i: (0, 0)),
            pl.BlockSpec((1, LANES), lambda i: (0, 0)),
            pl.BlockSpec((1, LANES), lambda i: (0, 0)),
        ],
        out_specs=[
            pl.BlockSpec((tb, GDN_QK_WIDTH), lambda i: (i, 0)),
            pl.BlockSpec((tb, GDN_QK_WIDTH), lambda i: (i, 0)),
            pl.BlockSpec((tb, MIX_WIDTH), lambda i: (i, 0)),
            pl.BlockSpec((tb, LANES), lambda i: (i, 0)),
            pl.BlockSpec((tb, LANES), lambda i: (i, 0)),
        ],
        out_shape=[
            jax.ShapeDtypeStruct((t, GDN_QK_WIDTH), F32),
            jax.ShapeDtypeStruct((t, GDN_QK_WIDTH), F32),
            jax.ShapeDtypeStruct((t, MIX_WIDTH), F32),
            jax.ShapeDtypeStruct((t, LANES), F32),
            jax.ShapeDtypeStruct((t, LANES), F32),
        ],
        scratch_shapes=[pltpu.VMEM((tb + SUBLANES, cw), F32)],
        compiler_params=_cparams("parallel"),
        name="gdn_prep",
    )(p, p, p, p, conv_w, alog_pad, dtb_pad)


GDN_PAIR = 2


def _gdn_chunk_kernel(q_ref, k_ref, v_ref, beta_ref, gcum_ref,
                      u_ref, w_ref, qg_ref, kd_ref, aqk_ref, egl_ref):
    c = GDN_CHUNK * GDN_PAIR
    shift = GDN_CHUNK.bit_length() - 1
    beta = beta_ref[...]
    gcum = gcum_ref[...]
    gcum_t = gcum.T
    for ck in range(GDN_PAIR):
        last = (ck + 1) * GDN_CHUNK - 1
        glast_col = gcum_t[:, last:last + 1]
        egl_ref[ck * N_MIX_HEADS:(ck + 1) * N_MIX_HEADS, :] = jnp.exp(
            jnp.broadcast_to(glast_col, (LANES, LANES))[0:N_MIX_HEADS, :])
    ri = lax.broadcasted_iota(jnp.int32, (c, c), 0)
    ci = lax.broadcasted_iota(jnp.int32, (c, c), 1)
    same = lax.shift_right_logical(ri, shift) == lax.shift_right_logical(ci, shift)
    causal = same & (ci <= ri)
    strict = same & (ci < ri)
    eye = jnp.where(ci == ri, 1.0, 0.0).astype(F32)
    row_chunk = lax.shift_right_logical(lax.broadcasted_iota(jnp.int32, (c, 1), 0), shift)
    rep = N_MIX_HEADS // GDN_QK_HEADS
    heads = range(N_MIX_HEADS)
    head_cols = [slice(h * LANES, (h + 1) * LANES) for h in heads]
    q_h = [q_ref[:, hq * LANES:(hq + 1) * LANES] for hq in range(GDN_QK_HEADS)]
    k_h = [k_ref[:, hq * LANES:(hq + 1) * LANES] for hq in range(GDN_QK_HEADS)]
    k_b = [k.astype(BF16) for k in k_h]
    qk = [_dot_nt(q.astype(BF16), kb16) for q, kb16 in zip(q_h, k_b)]
    g_col = [gcum[:, h:h + 1] for h in heads]
    b_col = [beta[:, h:h + 1] for h in heads]
    decay = [jnp.exp(jnp.where(causal, g_col[h] - gcum_t[h:h + 1, :], -jnp.inf)) for h in heads]
    eg = [jnp.exp(g) for g in g_col]
    kbeta = [k_h[h // rep] * b_col[h] for h in heads]
    kk = [_dot_nt(kbeta[h].astype(BF16), k_b[h // rep]) for h in heads]
    nm = [jnp.where(strict, -(kk[h] * decay[h]), 0.0) for h in heads]
    inv = [eye + n for n in nm]
    pw2 = [_split2(n) for n in nm]
    for _ in range(5):
        pw2 = [_split2(_dot3(p2, p2)) for p2 in pw2]
        inv = [inv[h] + _dot3(pw2[h], _split2(inv[h])) for h in heads]
    for h in heads:
        hd = head_cols[h]
        rhs = jnp.concatenate([v_ref[:, hd] * b_col[h], kbeta[h] * eg[h]], axis=1)
        sol = _dot3(_split2(inv[h]), _split2(rhs))
        u_ref[:, hd] = sol[:, :LANES]
        w_ref[:, hd] = sol[:, LANES:].astype(BF16)
        qg_ref[:, hd] = (q_h[h // rep] * eg[h]).astype(BF16)
        g_last = gcum[GDN_CHUNK - 1:GDN_CHUNK, h:h + 1]
        for ck in range(1, GDN_PAIR):
            last = (ck + 1) * GDN_CHUNK - 1
            g_last = jnp.where(row_chunk == ck, gcum[last:last + 1, h:h + 1], g_last)
        kd_ref[:, hd] = (k_h[h // rep] * jnp.exp(g_last - g_col[h])).astype(BF16)
        aqk = (qk[h // rep] * decay[h]).astype(BF16)
        for ck in range(GDN_PAIR):
            blk = slice(ck * GDN_CHUNK, (ck + 1) * GDN_CHUNK)
            aqk_ref[ck, h] = aqk[blk, blk]


def _gdn_chunk(qn, kn, vc, beta, gcum):
    t = qn.shape[0]
    c = GDN_CHUNK
    n = t // c
    rows = c * GDN_PAIR
    row = lambda w: pl.BlockSpec((rows, w), lambda i: (i, 0))
    return pl.pallas_call(
        _gdn_chunk_kernel,
        grid=(n // GDN_PAIR,),
        in_specs=[row(GDN_QK_WIDTH), row(GDN_QK_WIDTH), row(MIX_WIDTH), row(LANES), row(LANES)],
        out_specs=[row(MIX_WIDTH), row(MIX_WIDTH), row(MIX_WIDTH), row(MIX_WIDTH),
                   pl.BlockSpec((GDN_PAIR, N_MIX_HEADS, c, c), lambda i: (i, 0, 0, 0)),
                   pl.BlockSpec((GDN_PAIR * N_MIX_HEADS, LANES), lambda i: (i, 0))],
        out_shape=[
            jax.ShapeDtypeStruct((t, MIX_WIDTH), F32),
            jax.ShapeDtypeStruct((t, MIX_WIDTH), BF16),
            jax.ShapeDtypeStruct((t, MIX_WIDTH), BF16),
            jax.ShapeDtypeStruct((t, MIX_WIDTH), BF16),
            jax.ShapeDtypeStruct((n, N_MIX_HEADS, c, c), BF16),
            jax.ShapeDtypeStruct((n * N_MIX_HEADS, LANES), F32),
        ],
        compiler_params=_cparams("parallel"),
        name="gdn_chunk",
    )(qn, kn, vc, beta, gcum)


GDN_SCAN_CHUNKS = 8


def _gdn_scan_kernel(u_ref, w_ref, qg_ref, kd_ref, aqk_ref, egl_ref, z_ref, on_ref,
                     o_ref, s_ref):
    @pl.when(pl.program_id(0) == 0)
    def _():
        s_ref[...] = jnp.zeros_like(s_ref)

    on = on_ref[...]
    c = GDN_CHUNK
    heads = range(N_MIX_HEADS)
    cols = [slice(h * LANES, (h + 1) * LANES) for h in heads]
    state = [s_ref[h] for h in heads]
    for ck in range(GDN_SCAN_CHUNKS):
        rows = slice(ck * c, (ck + 1) * c)
        s_b = [s.astype(BF16) for s in state]
        ws = [_dot(w_ref[rows, cols[h]], s_b[h]) for h in heads]
        qs = [_dot(qg_ref[rows, cols[h]], s_b[h]) for h in heads]
        v_b = [(u_ref[rows, cols[h]] - ws[h]).astype(BF16) for h in heads]
        av = [_dot(aqk_ref[ck, h], v_b[h]) for h in heads]
        kv = [_dot_tn(kd_ref[rows, cols[h]], v_b[h]) for h in heads]
        egl = egl_ref[ck * N_MIX_HEADS:(ck + 1) * N_MIX_HEADS, :]
        state = [state[h] * egl[h:h + 1, :] + kv[h] for h in heads]
        for h in heads:
            o = qs[h] + av[h]
            o = o * lax.rsqrt(jnp.mean(o * o, axis=-1, keepdims=True) + RMS_EPS) * on
            o_ref[rows, cols[h]] = o * _silu(z_ref[rows, cols[h]])
    for h in heads:
        s_ref[h] = state[h]


def _gdn_scan(u, w, qg, kd, aqk, egl, p, onorm):
    t = u.shape[0]
    nck = GDN_SCAN_CHUNKS
    c = GDN_CHUNK * nck
    n = t // c
    row = lambda wd: pl.BlockSpec((c, wd), lambda i: (i, 0))
    return pl.pallas_call(
        _gdn_scan_kernel,
        grid=(n,),
        in_specs=[row(MIX_WIDTH), row(MIX_WIDTH), row(MIX_WIDTH), row(MIX_WIDTH),
                  pl.BlockSpec((nck, N_MIX_HEADS, GDN_CHUNK, GDN_CHUNK), lambda i: (i, 0, 0, 0)),
                  pl.BlockSpec((nck * N_MIX_HEADS, LANES), lambda i: (i, 0)),
                  pl.BlockSpec((c, MIX_WIDTH), lambda i: (i, GDN_COL_Z // MIX_WIDTH)),
                  pl.BlockSpec((1, LANES), lambda i: (0, 0))],
        out_specs=row(MIX_WIDTH),
        out_shape=jax.ShapeDtypeStruct((t, MIX_WIDTH), F32),
        scratch_shapes=[pltpu.VMEM((N_MIX_HEADS, HEAD_DIM, HEAD_DIM), F32)],
        compiler_params=_cparams("arbitrary"),
        name="gdn_scan",
    )(u, w, qg, kd, aqk, egl, p, onorm)


SB_SUBS = 8


def _sb_tiles(q_subs, k_ref, v_ref, kbs, c_runs, tri, keeps):
    blk = SB_BLOCK
    subs = range(len(q_subs))
    starts = [pl.multiple_of(kb * blk, blk) for kb in kbs]
    zs = [_dot_nt(q_subs[r], k_ref[pl.ds(starts[r], blk), :].astype(BF16)) * (HEAD_DIM ** -0.5)
          for r in subs]
    ts = [jnp.log1p(jnp.exp(-jnp.abs(z))) for z in zs]
    log_not = [jnp.where(keeps[r], -jnp.maximum(zs[r], 0.0) - ts[r], 0.0) for r in subs]
    log_beta = [jnp.minimum(zs[r], 0.0) - ts[r] for r in subs]
    hi = [x.astype(BF16) for x in log_not]
    rest = [log_not[r] - hi[r].astype(F32) for r in subs]
    mid = [x.astype(BF16) for x in rest]
    lo = [(rest[r] - mid[r].astype(F32)).astype(BF16) for r in subs]
    suffix = [_dot(hi[r], tri) + (_dot(mid[r], tri) + _dot(lo[r], tri)) for r in subs]
    att = [jnp.where(keeps[r], jnp.exp(log_beta[r] + suffix[r] + c_runs[r]), 0.0).astype(BF16) for r in subs]
    contrib = [_dot(att[r], v_ref[pl.ds(starts[r], blk), :].astype(BF16)) for r in subs]
    c_new = [c_runs[r] + jnp.sum(log_not[r], axis=-1, keepdims=True) for r in subs]
    return c_new, contrib


def _sb_kernel(q_ref, k_ref, v_ref, o_ref):
    blk, subs = SB_BLOCK, SB_SUBS
    qb = pl.program_id(1)
    ri = lax.broadcasted_iota(jnp.int32, (blk, blk), 0)
    ci = lax.broadcasted_iota(jnp.int32, (blk, blk), 1)
    tri = jnp.where(ri > ci, 1.0, 0.0).astype(BF16)
    q_subs = [q_ref[r * blk:(r + 1) * blk, :].astype(BF16) for r in range(subs)]

    def all_max(cs):
        m = cs[0]
        for c in cs[1:]:
            m = jnp.maximum(m, c)
        return jnp.max(m)

    cs, contrib = _sb_tiles(q_subs, k_ref, v_ref, [qb * subs + r for r in range(subs)],
                            [jnp.zeros((blk, 1), F32)] * subs, tri, [ci < ri] * subs)
    for r in range(subs):
        o_ref[r * blk:(r + 1) * blk, :] = contrib[r]

    last = qb * subs + subs - 1

    def cond(carry):
        return jnp.logical_and(carry[0] <= last, carry[1])

    def body(carry):
        n = carry[0]
        kbs = [qb * subs + r - n for r in range(subs)]
        new, contrib = _sb_tiles(q_subs, k_ref, v_ref, [jnp.maximum(kb, 0) for kb in kbs],
                                 list(carry[2:]), tri, [kb >= 0 for kb in kbs])
        for r in range(subs):
            o_ref[r * blk:(r + 1) * blk, :] += contrib[r]
        return (n + 1, all_max(new) > EXP_ZERO_BELOW, *new)

    lax.while_loop(cond, body, (jnp.int32(1), all_max(cs) > EXP_ZERO_BELOW, *cs))


def _stick_breaking(p):
    t = p.shape[0]
    blk = SB_BLOCK * SB_SUBS
    return pl.pallas_call(
        _sb_kernel,
        grid=(N_MIX_HEADS, t // blk),
        in_specs=[pl.BlockSpec((blk, HEAD_DIM), lambda h, i: (i, h)),
                  pl.BlockSpec((t, HEAD_DIM), lambda h, i: (0, N_MIX_HEADS + h)),
                  pl.BlockSpec((t, HEAD_DIM), lambda h, i: (0, 2 * N_MIX_HEADS + h))],
        out_specs=pl.BlockSpec((blk, HEAD_DIM), lambda h, i: (i, h)),
        out_shape=jax.ShapeDtypeStruct((t, MIX_WIDTH), F32),
        compiler_params=_cparams("parallel", "parallel"),
        name="stick_breaking",
    )(p, p, p)


OUT_ROWS = 512


def _mem_attention(q_ref, kv_ref):
    heads = []
    for h in range(MEM_HEADS):
        hd = slice(h * LANES, (h + 1) * LANES)
        km = kv_ref[:, hd].astype(BF16)
        vm = kv_ref[:, MEM_WIDTH + h * LANES:MEM_WIDTH + (h + 1) * LANES].astype(BF16)
        s = _dot_nt(q_ref[:, hd].astype(BF16), km) * (HEAD_DIM ** -0.5)
        s = s - jnp.max(s, axis=-1, keepdims=True)
        e = jnp.exp(s)
        prob = e / jnp.sum(e, axis=-1, keepdims=True)
        heads.append(_dot(prob.astype(BF16), vm).astype(BF16))
    return jnp.concatenate(heads, axis=1)


def _out_ln_kernel(mix_ref, q_ref, kv_ref, x_ref, w_ref, g_ref, b_ref, o_ref, ot_ref):
    y = _dot(mix_ref[...].astype(BF16), w_ref[0:MIX_WIDTH, :])
    y = y + _dot(_mem_attention(q_ref, kv_ref), w_ref[MIX_WIDTH:, :])
    o = _layer_norm(DN_ALPHA * x_ref[...] + y, g_ref[...], b_ref[...])
    o_ref[...] = o
    ot_ref[...] = o.T.astype(BF16)


def _out_ln(mix, p, qm_col, kv, x, w_out, g, b):
    t, d = x.shape
    tb = OUT_ROWS
    return pl.pallas_call(
        _out_ln_kernel,
        grid=(t // tb,),
        in_specs=[pl.BlockSpec((tb, MIX_WIDTH), lambda i: (i, 0)),
                  pl.BlockSpec((tb, MEM_WIDTH), lambda i: (i, qm_col // MEM_WIDTH)),
                  pl.BlockSpec((N_MEM, 2 * MEM_WIDTH), lambda i: (0, 0)),
                  pl.BlockSpec((tb, d), lambda i: (i, 0)),
                  pl.BlockSpec((MIX_WIDTH + MEM_WIDTH, d), lambda i: (0, 0)),
                  pl.BlockSpec((1, d), lambda i: (0, 0)),
                  pl.BlockSpec((1, d), lambda i: (0, 0))],
        out_specs=[pl.BlockSpec((tb, d), lambda i: (i, 0)),
                   pl.BlockSpec((d, tb), lambda i: (0, i))],
        out_shape=[jax.ShapeDtypeStruct((t, d), F32),
                   jax.ShapeDtypeStruct((d, t), BF16)],
        compiler_params=_cparams("parallel"),
        name="out_proj_ln",
    )(mix, p, kv, x, w_out, g, b)


ROUTE_COLS = 512
PEER_HALF_KEEP = PEER_TOPK
assert PEER_HALF_KEEP <= PEER_HALF_TOPK
PEER_NCAND = PEER_HALF_KEEP * PEER_HALF_KEEP
PEER_SCORE_ROWS = PEER_HEADS * 2 * PEER_NKEYS
AUX_ROWS = 16
AUX_THR = 0
AUX_OFF = PEER_HEADS
AUX_TIE = 2 * PEER_HEADS


def _top_rows(vals, row_idx, count, limit, emit):
    for r in range(count):
        m = jnp.max(vals, axis=0, keepdims=True)
        am = jnp.min(jnp.where(vals == m, row_idx, float(limit)), axis=0, keepdims=True)
        sel = row_idx == am
        emit(r, m, am, sel)
        vals = jnp.where(sel, -jnp.inf, vals)
    return jnp.max(vals, axis=0, keepdims=True)


def _route_kernel(xt_ref, wq_ref, keys_ref, sc_ref, aux_ref, ids_ref, gates_ref,
                  sv_ref, ix_ref, cand_ref, eid_ref):
    tb = ROUTE_COLS
    qry = _dot(wq_ref[...], xt_ref[...])
    row_k = lax.broadcasted_iota(jnp.int32, (PEER_NKEYS, tb), 0).astype(F32)
    row_c = lax.broadcasted_iota(jnp.int32, (PEER_NCAND, tb), 0).astype(F32)
    tie = jnp.zeros((1, tb), F32)
    for h in range(PEER_HEADS):
        left = []
        for half in range(2):
            r0 = (h * 2 + half) * PEER_NKEYS
            s = _dot(keys_ref[half], qry[r0:r0 + PEER_NKEYS, :].astype(BF16))
            sc_ref[r0:r0 + PEER_NKEYS, :] = s

            def emit_half(r, m, am, sel, half=half):
                sv_ref[half, r:r + 1, :] = m
                ix_ref[half, r:r + 1, :] = am

            left.append(_top_rows(s, row_k, PEER_HALF_KEEP, PEER_NKEYS, emit_half))
        sv1, sv2 = sv_ref[0], sv_ref[1]
        ix1, ix2 = ix_ref[0], ix_ref[1]
        for a in range(PEER_HALF_KEEP):
            rows = slice(a * PEER_HALF_KEEP, (a + 1) * PEER_HALF_KEEP)
            cand_ref[rows, :] = sv1[a:a + 1, :] + sv2
            eid_ref[rows, :] = ix1[a:a + 1, :] * float(PEER_NKEYS) + ix2
        eid = eid_ref[...]
        cvs, eids = [], []

        def emit_cand(r, m, am, sel):
            cvs.append(m)
            eids.append(jnp.max(jnp.where(sel, eid, -1.0), axis=0, keepdims=True))

        runner_up = _top_rows(cand_ref[...], row_c, PEER_TOPK, PEER_NCAND, emit_cand)
        thr = cvs[PEER_TOPK - 1]
        outside = jnp.maximum(left[0] + sv2[0:1, :], sv1[0:1, :] + left[1])
        tie = jnp.maximum(tie, jnp.where(jnp.maximum(outside, runner_up) >= thr, 1.0, 0.0))
        exps = [jnp.exp(cv - cvs[0]) for cv in cvs]
        denom = exps[0]
        for e in exps[1:]:
            denom = denom + e
        aux_ref[AUX_THR + h:AUX_THR + h + 1, :] = thr
        aux_ref[AUX_OFF + h:AUX_OFF + h + 1, :] = -(cvs[0] + jnp.log(denom))
        for r in range(PEER_TOPK):
            slot = h * PEER_TOPK + r
            ids_ref[slot:slot + 1, :] = eids[r].astype(jnp.int32)
            gates_ref[slot:slot + 1, :] = exps[r] / denom
    aux_ref[AUX_TIE:AUX_TIE + 1, :] = tie
    aux_ref[AUX_TIE + 1:, :] = jnp.zeros((AUX_ROWS - AUX_TIE - 1, tb), F32)


def _route(xt, wq_t, keys):
    d, t = xt.shape
    tb = ROUTE_COLS
    col = lambda rows: pl.BlockSpec((rows, tb), lambda i: (0, i))
    return pl.pallas_call(
        _route_kernel,
        grid=(t // tb,),
        in_specs=[col(d),
                  pl.BlockSpec((PEER_SCORE_ROWS, d), lambda i: (0, 0)),
                  pl.BlockSpec((2, PEER_NKEYS, PEER_HALF), lambda i: (0, 0, 0))],
        out_specs=[col(PEER_SCORE_ROWS), col(AUX_ROWS), col(PEER_SLOTS), col(PEER_SLOTS)],
        out_shape=[jax.ShapeDtypeStruct((PEER_SCORE_ROWS, t), F32),
                   jax.ShapeDtypeStruct((AUX_ROWS, t), F32),
                   jax.ShapeDtypeStruct((PEER_SLOTS, t), jnp.int32),
                   jax.ShapeDtypeStruct((PEER_SLOTS, t), F32)],
        scratch_shapes=[pltpu.VMEM((2, PEER_HALF_KEEP, tb), F32),
                        pltpu.VMEM((2, PEER_HALF_KEEP, tb), F32),
                        pltpu.VMEM((PEER_NCAND, tb), F32),
                        pltpu.VMEM((PEER_NCAND, tb), F32)],
        compiler_params=_cparams("parallel"),
        name="peer_route",
    )(xt, wq_t, keys)


PEER_COLS = 512
PEER_ECHUNK = 1024

def _peer_kernel(xt_ref, x_ref, sc_ref, aux_ref, ids_ref, gates_ref, u_ref, vt_ref, g_ref, b_ref,
                 o_ref, ob_ref, acc_ref):
    j = pl.program_id(1)
    tb, ec = PEER_COLS, PEER_ECHUNK
    groups = ec // PEER_NKEYS

    @pl.when(j == 0)
    def _():
        acc_ref[...] = jnp.zeros_like(acc_ref)

    def threshold_gates(gi):
        w = jnp.zeros((PEER_NKEYS, tb), F32)
        for h in range(PEER_HEADS):
            base = h * 2 * PEER_NKEYS
            s1 = sc_ref[pl.ds(base + j * groups + gi, 1), :]
            s2 = sc_ref[base + PEER_NKEYS:base + 2 * PEER_NKEYS, :]
            dsum = s1 + s2
            gate = jnp.exp(dsum + aux_ref[AUX_OFF + h:AUX_OFF + h + 1, :])
            w = w + jnp.where(dsum >= aux_ref[AUX_THR + h:AUX_THR + h + 1, :], gate, 0.0)
        return w

    def scattered_gates(gi):
        row_e = lax.broadcasted_iota(jnp.int32, (PEER_NKEYS, tb), 0) + (j * ec + gi * PEER_NKEYS)
        w = jnp.zeros((PEER_NKEYS, tb), F32)
        for s in range(PEER_SLOTS):
            w = w + jnp.where(ids_ref[s:s + 1, :] == row_e, gates_ref[s:s + 1, :], 0.0)
        return w

    def step(gates_of):
        xt = xt_ref[...]
        acts = []
        for gi in range(groups):
            rows = slice(gi * PEER_NKEYS, (gi + 1) * PEER_NKEYS)
            hid = _dot(u_ref[rows, :], xt)
            gelu = 0.5 * hid * (1.0 + lax.erf(hid * (2.0 ** -0.5)))
            acts.append((gelu * gates_of(gi)).astype(BF16))
        acc_ref[...] += _dot(vt_ref[...], jnp.concatenate(acts, axis=0))

    any_tie = jnp.max(aux_ref[AUX_TIE:AUX_TIE + 1, :]) > 0.0
    pl.when(jnp.logical_not(any_tie))(lambda: step(threshold_gates))
    pl.when(any_tie)(lambda: step(scattered_gates))

    @pl.when(j == pl.num_programs(1) - 1)
    def _():
        o = _layer_norm(DN_ALPHA * x_ref[...] + acc_ref[...].T, g_ref[...], b_ref[...])
        o_ref[...] = o
        ob_ref[...] = o.astype(BF16)


def _peer(xt, x, scores, aux, ids, gates, u_all, vt_all, layer, g, b):
    t, d = x.shape
    tb, ec = PEER_COLS, PEER_ECHUNK
    ne = u_all.shape[1]
    col = lambda rows: pl.BlockSpec((rows, tb), lambda i, j: (0, i))
    return pl.pallas_call(
        _peer_kernel,
        grid=(t // tb, ne // ec),
        in_specs=[col(d),
                  pl.BlockSpec((tb, d), lambda i, j: (i, 0)),
                  col(PEER_SCORE_ROWS), col(AUX_ROWS), col(PEER_SLOTS), col(PEER_SLOTS),
                  pl.BlockSpec((None, ec, d), lambda i, j: (layer, j, 0)),
                  pl.BlockSpec((None, d, ec), lambda i, j: (layer, 0, j)),
                  pl.BlockSpec((1, d), lambda i, j: (0, 0)),
                  pl.BlockSpec((1, d), lambda i, j: (0, 0))],
        out_specs=[pl.BlockSpec((tb, d), lambda i, j: (i, 0)),
                   pl.BlockSpec((tb, d), lambda i, j: (i, 0))],
        out_shape=[jax.ShapeDtypeStruct((t, d), F32), jax.ShapeDtypeStruct((t, d), BF16)],
        scratch_shapes=[pltpu.VMEM((d, tb), F32)],
        compiler_params=_cparams("parallel", "arbitrary"),
        name="peer_dense",
    )(xt, x, scores, aux, ids, gates, u_all, vt_all, g, b)


def _pad_lanes(vec):
    return jnp.zeros((1, LANES), F32).at[0, :vec.shape[0]].set(vec.astype(F32))


def _gdn_weight(w):
    qw, vw, nh = GDN_QK_WIDTH, MIX_WIDTH, N_MIX_HEADS
    d = w.shape[0]
    o_b = 2 * qw + 2 * vw
    main = w[:, :o_b]
    b_cols = w[:, o_b:o_b + nh]
    a_cols = w[:, o_b + nh:o_b + 2 * nh]
    qm = w[:, o_b + 2 * nh:]
    zpad = jnp.zeros((d, LANES - nh), w.dtype)
    return jnp.concatenate([main, qm, b_cols, zpad, a_cols, zpad], axis=1).astype(BF16)


def _forward(x, mem, gdn_w_in, gdn_conv, gdn_a_log, gdn_dt_bias, gdn_onorm, sb_w_in, mem_w_kv, w_out,
             ln_mix_g, ln_mix_b, peer_w_q, peer_keys, peer_u, peer_v, ln_ffn_g, ln_ffn_b):
    t = x.shape[1]
    d = x.shape[2]
    xt = x.reshape(t, d)
    x_op = xt
    memt = mem.reshape(N_MEM, d)
    row = lambda a: a.reshape(1, -1).astype(F32)
    u_all = peer_u.astype(BF16)
    vt_all = jnp.swapaxes(peer_v, 1, 2).astype(BF16)
    for i in range(DEPTH):
        li = i // 2
        if i % 2 == 0:
            p = _matmul(x_op, _gdn_weight(gdn_w_in[li]), 1024, 1920)
            qn, kn, vc, beta, gcum = _gdn_prep(p, gdn_conv[li].astype(F32),
                                               _pad_lanes(gdn_a_log[li]), _pad_lanes(gdn_dt_bias[li]))
            u, w, qg, kd, aqk, egl = _gdn_chunk(qn, kn, vc, beta, gcum)
            mix = _gdn_scan(u, w, qg, kd, aqk, egl, p, row(gdn_onorm[li]))
            qm_col = GDN_COL_QM
        else:
            p = _matmul(x_op, sb_w_in[li].astype(BF16), 1024, 1792, BF16)
            mix = _stick_breaking(p)
            qm_col = SB_COL_QM
        kv = _matmul(memt, mem_w_kv[i].astype(BF16), N_MEM, 512)
        x1, x1_t = _out_ln(mix, p, qm_col, kv, xt, w_out[i].astype(BF16), row(ln_mix_g[i]), row(ln_mix_b[i]))
        scores, aux, ids, gates = _route(x1_t, peer_w_q[i].T.astype(BF16), peer_keys[i].astype(BF16))
        xt, x_op = _peer(x1_t, x1, scores, aux, ids, gates, u_all, vt_all, i, row(ln_ffn_g[i]), row(ln_ffn_b[i]))
    return xt.reshape(x.shape)


def kernel(x, mem, gdn_w_in, gdn_conv, gdn_a_log, gdn_dt_bias, gdn_onorm, sb_w_in, mem_w_kv, w_out,
           ln_mix_g, ln_mix_b, peer_w_q, peer_keys, peer_u, peer_v, ln_ffn_g, ln_ffn_b):
    return _forward(x, mem, gdn_w_in, gdn_conv, gdn_a_log, gdn_dt_bias, gdn_onorm, sb_w_in, mem_w_kv,
                    w_out, ln_mix_g, ln_mix_b, peer_w_q, peer_keys, peer_u, peer_v, ln_ffn_g, ln_ffn_b)
```
